```python
import math
import jax, jax.numpy as jnp
from jax import lax
import numpy as np

D_MODEL = 1024
BATCH = 1
SEQ = 16384
DEPTH = 2
DEC_BATCH = 32
DEC_SEQ = 4
PAST_LEN = 16384
PAGE_SIZE = 128

N_MIXERS = 2
N_ATTN_LAYERS = (DEPTH + 1) // 2
N_S5_LAYERS = DEPTH // 2
N_HEADS = 16
HEAD_DIM = D_MODEL // N_HEADS
ATTN_WIDTH = N_HEADS * HEAD_DIM
Q_BLOCK = 128
FORGET_BIAS_INIT = 4.0
S5_WIDTH = D_MODEL
S5_GROUP = 16
S5_GROUPS = S5_WIDTH // S5_GROUP
S5_STATE = 64
DT_MIN = 0.001
DT_MAX = 0.1
ALPHA = (2 * DEPTH) ** 0.25
BETA = (8 * DEPTH) ** -0.25
LN_EPS = 1e-5

kernel_name = "fox_s5_interleaved_decoder_step"

F32 = jnp.float32


def layer_norm(x, g, b):
    xf = x.astype(F32)
    mu = jnp.mean(xf, axis=-1, keepdims=True)
    var = jnp.mean(jnp.square(xf - mu), axis=-1, keepdims=True)
    return ((xf - mu) * lax.rsqrt(var + LN_EPS) * g.astype(F32) + b.astype(F32)).astype(x.dtype)


def fox_project(x, w_in, b_f):
    bsz, l = x.shape[:2]
    h = jnp.einsum('bld,de->ble', x, w_in)
    q, k, v, z = jnp.split(h[..., :4 * ATTN_WIDTH], 4, axis=-1)
    logf = jax.nn.log_sigmoid((h[..., 4 * ATTN_WIDTH:] + b_f).astype(F32))
    shp = (bsz, l, N_HEADS, HEAD_DIM)
    return q.reshape(shp), k.reshape(shp), v.reshape(shp), z, logf


def fox_attend_prompt(q, k, v, logf):
    bsz, l = q.shape[:2]
    scale = HEAD_DIM ** -0.5
    n_blocks = l // Q_BLOCK
    c_t = jnp.cumsum(logf, axis=1).transpose(0, 2, 1)
    qb = q.reshape(bsz, n_blocks, Q_BLOCK, N_HEADS, HEAD_DIM).swapaxes(0, 1)
    cb = c_t.reshape(bsz, N_HEADS, n_blocks, Q_BLOCK).transpose(2, 0, 1, 3)
    pos_k = jnp.arange(l)

    def block(args):
        i, q_i, c_i = args
        s = jnp.einsum('bqhd,bkhd->bhqk', q_i, k).astype(F32) * scale
        s = s + (c_i[..., :, None] - c_t[..., None, :])
        pos_q = i * Q_BLOCK + jnp.arange(Q_BLOCK)
        s = jnp.where(pos_k[None, :] <= pos_q[:, None], s, -jnp.inf)
        p = jax.nn.softmax(s, axis=-1)
        return jnp.einsum('bhqk,bkhd->bqhd', p.astype(v.dtype), v)

    o = lax.map(block, (jnp.arange(n_blocks), qb, cb))
    return o.swapaxes(0, 1).reshape(bsz, l, ATTN_WIDTH)


def fox_attend_sample(q, k, v, logf, cache_k_l, cache_v_l, cache_logf_l, page_table):
    bd, t = q.shape[:2]
    n_pages = page_table.shape[1]
    scale = HEAD_DIM ** -0.5
    logf_past = cache_logf_l[page_table].reshape(bd, n_pages * PAGE_SIZE, N_HEADS).astype(F32)
    c_past = jnp.cumsum(logf_past, axis=1)
    c_new = (c_past[:, -1:, :] + jnp.cumsum(logf, axis=1)).transpose(0, 2, 1)
    c_pages = c_past.reshape(bd, n_pages, PAGE_SIZE, N_HEADS).transpose(1, 0, 3, 2)

    def page_step(carry, xs):
        m, l_sum, acc = carry
        phys, c_pg = xs
        k_pg = cache_k_l[phys]
        v_pg = cache_v_l[phys]
        s = jnp.einsum('bqhd,bkhd->bhqk', q, k_pg).astype(F32) * scale
        s = s + (c_new[..., :, None] - c_pg[..., None, :])
        m_new = jnp.maximum(m, jnp.max(s, axis=-1))
        corr = jnp.exp(m - m_new)
        p = jnp.exp(s - m_new[..., None])
        l_sum = l_sum * corr + jnp.sum(p, axis=-1)
        acc = acc * corr[..., None] + jnp.einsum('bhqk,bkhd->bhqd', p, v_pg.astype(F32))
        return (m_new, l_sum, acc), None

    init = (jnp.full((bd, N_HEADS, t), -jnp.inf, F32),
            jnp.zeros((bd, N_HEADS, t), F32),
            jnp.zeros((bd, N_HEADS, t, HEAD_DIM), F32))
    (m, l_sum, acc), _ = lax.scan(page_step, init, (page_table.T, c_pages))

    s = jnp.einsum('bqhd,bkhd->bhqk', q, k).astype(F32) * scale
    s = s + (c_new[..., :, None] - c_new[..., None, :])
    s = jnp.where(jnp.tril(jnp.ones((t, t), dtype=bool)), s, -jnp.inf)
    m_new = jnp.maximum(m, jnp.max(s, axis=-1))
    corr = jnp.exp(m - m_new)
    p = jnp.exp(s - m_new[..., None])
    l_sum = l_sum * corr + jnp.sum(p, axis=-1)
    acc = acc * corr[..., None] + jnp.einsum('bhqk,bkhd->bhqd', p, v.astype(F32))
    o = acc / l_sum[..., None]
    return o.transpose(0, 2, 1, 3).reshape(bd, t, ATTN_WIDTH).astype(q.dtype)


def gated_out(o, z, w_out):
    return jnp.einsum('ble,ed->bld', o * jax.nn.silu(z), w_out)


def s5_layer(x, h0_re, h0_im, w_in, a_re, a_im, log_dt, b_re, b_im, c_re, c_im, d,
             w_glu, b_glu, w_out):
    bsz, l = x.shape[:2]
    u, z = jnp.split(jnp.einsum('bld,de->ble', x, w_in), 2, axis=-1)
    ug = u.reshape(bsz, l, S5_GROUPS, S5_GROUP).astype(F32)
    lam = lax.complex(a_re.astype(F32), a_im.astype(F32))
    dt = jnp.exp(log_dt.astype(F32))[:, None]
    a_bar = jnp.exp(lam * dt)
    b_bar = ((a_bar - 1.0) / lam)[..., None] * lax.complex(b_re.astype(F32), b_im.astype(F32))
    bu = jnp.einsum('gph,blgh->blgp', b_bar, ug.astype(jnp.complex64))
    h0 = lax.complex(h0_re.astype(F32), h0_im.astype(F32))
    bu = bu.at[:, 0].add(a_bar * h0)

    def combine(e1, e2):
        a1, x1 = e1
        a2, x2 = e2
        return a1 * a2, a2 * x1 + x2

    _, h = lax.associative_scan(combine, (jnp.broadcast_to(a_bar, bu.shape), bu), axis=1)
    y = (jnp.einsum('ghp,blgp->blgh', c_re.astype(F32), jnp.real(h))
         - jnp.einsum('ghp,blgp->blgh', c_im.astype(F32), jnp.imag(h))
         + d.astype(F32) * ug)
    y = y.reshape(bsz, l, S5_WIDTH).astype(x.dtype)
    g = jax.nn.gelu(y)
    y = g * jax.nn.sigmoid(jnp.einsum('ble,ef->blf', g, w_glu) + b_glu)
    h_last = h[:, -1]
    return gated_out(y, z, w_out), jnp.real(h_last), jnp.imag(h_last)


def setup_inputs(seed: int = 0) -> dict:
    key = jax.random.key(seed)
    ks = jax.random.split(key, 32)
    n_pages = PAST_LEN // PAGE_SIZE
    n_used = DEC_BATCH * n_pages
    n_pool = n_used + (n_used + 3) // 4

    def nrm(k, shape, scale):
        return jax.random.normal(k, shape, F32) * scale

    inp = {}
    inp['x_prompt'] = nrm(ks[0], (BATCH, SEQ, D_MODEL), 1.0)
    inp['x_sample'] = nrm(ks[1], (DEC_BATCH, DEC_SEQ, D_MODEL), 1.0)
    inp['cache_k'] = nrm(ks[2], (N_ATTN_LAYERS, n_pool, PAGE_SIZE, N_HEADS, HEAD_DIM), 1.0)
    inp['cache_v'] = nrm(ks[3], (N_ATTN_LAYERS, n_pool, PAGE_SIZE, N_HEADS, HEAD_DIM), 1.0)
    inp['cache_logf'] = jax.nn.log_sigmoid(
        FORGET_BIAS_INIT + nrm(ks[4], (N_ATTN_LAYERS, n_pool, PAGE_SIZE, N_HEADS), 1.0))
    inp['page_table'] = jax.random.permutation(ks[5], n_pool)[:n_used].reshape(
        DEC_BATCH, n_pages).astype(jnp.int32)
    inp['state_s5_re'] = nrm(ks[6], (N_S5_LAYERS, DEC_BATCH, S5_GROUPS, S5_STATE), 0.1)
    inp['state_s5_im'] = nrm(ks[7], (N_S5_LAYERS, DEC_BATCH, S5_GROUPS, S5_STATE), 0.1)
    inp['attn_w_in'] = nrm(ks[8], (N_ATTN_LAYERS, D_MODEL, 4 * ATTN_WIDTH + N_HEADS), D_MODEL ** -0.5)
    inp['attn_b_f'] = FORGET_BIAS_INIT + nrm(ks[9], (N_ATTN_LAYERS, N_HEADS), 0.1)
    inp['attn_w_out'] = nrm(ks[10], (N_ATTN_LAYERS, ATTN_WIDTH, D_MODEL), BETA * ATTN_WIDTH ** -0.5)
    inp['s5_w_in'] = nrm(ks[11], (N_S5_LAYERS, D_MODEL, 2 * S5_WIDTH), D_MODEL ** -0.5)
    inp['s5_a_re'] = -0.5 + nrm(ks[12], (N_S5_LAYERS, S5_GROUPS, S5_STATE), 0.01)
    inp['s5_a_im'] = (math.pi * jnp.arange(S5_STATE, dtype=F32)
                      + nrm(ks[13], (N_S5_LAYERS, S5_GROUPS, S5_STATE), 0.01))
    inp['s5_log_dt'] = jax.random.uniform(ks[14], (N_S5_LAYERS, S5_GROUPS), F32,
                                          minval=math.log(DT_MIN), maxval=math.log(DT_MAX))
    inp['s5_b_re'] = nrm(ks[15], (N_S5_LAYERS, S5_GROUPS, S5_STATE, S5_GROUP), (2 * S5_GROUP) ** -0.5)
    inp['s5_b_im'] = nrm(ks[16], (N_S5_LAYERS, S5_GROUPS, S5_STATE, S5_GROUP), (2 * S5_GROUP) ** -0.5)
    inp['s5_c_re'] = nrm(ks[17], (N_S5_LAYERS, S5_GROUPS, S5_GROUP, S5_STATE), S5_STATE ** -0.5)
    inp['s5_c_im'] = nrm(ks[18], (N_S5_LAYERS, S5_GROUPS, S5_GROUP, S5_STATE), S5_STATE ** -0.5)
    inp['s5_d'] = nrm(ks[19], (N_S5_LAYERS, S5_GROUPS, S5_GROUP), 1.0)
    inp['s5_w_glu'] = nrm(ks[20], (N_S5_LAYERS, S5_WIDTH, S5_WIDTH), S5_WIDTH ** -0.5)
    inp['s5_b_glu'] = nrm(ks[21], (N_S5_LAYERS, S5_WIDTH), 0.01)
    inp['s5_w_out'] = nrm(ks[22], (N_S5_LAYERS, S5_WIDTH, D_MODEL), BETA * S5_WIDTH ** -0.5)
    inp['ln_g'] = 1.0 + nrm(ks[23], (DEPTH, D_MODEL), 0.01)
    inp['ln_b'] = nrm(ks[24], (DEPTH, D_MODEL), 0.01)
    return inp


def reference(x_prompt, x_sample, cache_k, cache_v, cache_logf, page_table, state_s5_re, state_s5_im,
              attn_w_in, attn_b_f, attn_w_out, s5_w_in, s5_a_re, s5_a_im, s5_log_dt,
              s5_b_re, s5_b_im, s5_c_re, s5_c_im, s5_d, s5_w_glu, s5_b_glu, s5_w_out,
              ln_g, ln_b):

    def run(x, attend, h0_re_all, h0_im_all):
        kv_rows, s5_states = [], []
        for i in range(DEPTH):
            j = i // N_MIXERS
            if i % N_MIXERS == 0:
                q, k, v, z, logf = fox_project(x, attn_w_in[j], attn_b_f[j])
                sub = gated_out(attend(j, q, k, v, logf), z, attn_w_out[j])
                kv_rows.append((k, v, logf))
            else:
                sub, h_re, h_im = s5_layer(x, h0_re_all[j], h0_im_all[j], s5_w_in[j], s5_a_re[j],
                                           s5_a_im[j], s5_log_dt[j], s5_b_re[j], s5_b_im[j],
                                           s5_c_re[j], s5_c_im[j], s5_d[j], s5_w_glu[j],
                                           s5_b_glu[j], s5_w_out[j])
                s5_states.append((h_re, h_im))
            x = layer_norm(ALPHA * x + sub, ln_g[i], ln_b[i])
        return x, kv_rows, s5_states

    def attend_prompt(j, q, k, v, logf):
        return fox_attend_prompt(q, k, v, logf)

    def attend_sample(j, q, k, v, logf):
        return fox_attend_sample(q, k, v, logf, cache_k[j], cache_v[j], cache_logf[j], page_table)

    zeros_state = jnp.zeros((N_S5_LAYERS, x_prompt.shape[0], S5_GROUPS, S5_STATE), F32)
    y_prompt, kv_p, s5_p = run(x_prompt, attend_prompt, zeros_state, zeros_state)
    y_sample, kv_s, s5_s = run(x_sample, attend_sample, state_s5_re, state_s5_im)

    k_prompt = jnp.stack([r[0] for r in kv_p])
    v_prompt = jnp.stack([r[1] for r in kv_p])
    logf_prompt = jnp.stack([r[2] for r in kv_p])
    k_sample = jnp.stack([r[0] for r in kv_s])
    v_sample = jnp.stack([r[1] for r in kv_s])
    logf_sample = jnp.stack([r[2] for r in kv_s])
    s5_re_prompt = jnp.stack([r[0] for r in s5_p])
    s5_im_prompt = jnp.stack([r[1] for r in s5_p])
    s5_re_sample = jnp.stack([r[0] for r in s5_s])
    s5_im_sample = jnp.stack([r[1] for r in s5_s])
    return (y_prompt, y_sample, k_prompt, v_prompt, logf_prompt, k_sample, v_sample, logf_sample,
            s5_re_prompt, s5_im_prompt, s5_re_sample, s5_im_sample)
```

```python
import functools
import math

import jax
import jax.numpy as jnp
from jax import lax
from jax.experimental import pallas as pl
from jax.experimental.pallas import tpu as pltpu

F32 = jnp.float32
BF16 = jnp.bfloat16

D_MODEL = 1024
N_HEADS = 16
HEAD_DIM = 64
PAGE = 128
DEPTH = 2
ALPHA = (2 * DEPTH) ** 0.25
LN_EPS = 1e-5
QK_SCALE = HEAD_DIM ** -0.5

LANES = 128
N_SLABS = D_MODEL // LANES
S5_GROUP = 16
S5_STATE = 64
GROUPS_PER_SLAB = LANES // S5_GROUP
SLAB_STATE = GROUPS_PER_SLAB * S5_STATE
S5_CHUNK = 64
VMEM_LIMIT = 56 * 1024 * 1024

NT_DIMS = (((1,), (1,)), ((), ()))


def _cparams(sem):
    return pltpu.CompilerParams(dimension_semantics=sem, vmem_limit_bytes=VMEM_LIMIT)


def _split2(x):
    hi = x.astype(BF16)
    lo = (x - hi.astype(F32)).astype(BF16)
    return hi, lo


def _split3(x):
    hi = x.astype(BF16)
    r = x - hi.astype(F32)
    mid = r.astype(BF16)
    lo = (r - mid.astype(F32)).astype(BF16)
    return hi, mid, lo


def _layer_norm(r, g, b):
    mu = jnp.mean(r, axis=-1, keepdims=True)
    c = r - mu
    var = jnp.mean(c * c, axis=-1, keepdims=True)
    return c * lax.rsqrt(var + LN_EPS) * g + b


def _silu(z):
    return z * jax.nn.sigmoid(z)


def _attn_in_kernel(x_ref, w_ref, wf_ref, bf_ref, q_ref, k_ref, v_ref, kb_ref, vb_ref, z_ref, lf_ref):
    xb = x_ref[...].astype(BF16)
    q = jnp.dot(xb, w_ref[:, 0:D_MODEL], preferred_element_type=F32)
    q_ref[...] = (q * QK_SCALE).astype(BF16)
    k = jnp.dot(xb, w_ref[:, D_MODEL:2 * D_MODEL], preferred_element_type=F32)
    k_ref[...] = k
    kb_ref[...] = k.astype(BF16)
    v = jnp.dot(xb, w_ref[:, 2 * D_MODEL:3 * D_MODEL], preferred_element_type=F32)
    v_ref[...] = v
    vb_ref[...] = v.astype(BF16)
    z_ref[...] = jnp.dot(xb, w_ref[:, 3 * D_MODEL:4 * D_MODEL], preferred_element_type=F32)
    hf = jnp.dot(xb, wf_ref[...], preferred_element_type=F32) + bf_ref[...]
    lf = jnp.minimum(hf, 0.0) - jnp.log(1.0 + jnp.exp(-jnp.abs(hf)))
    lf_ref[...] = lf[:, 0:N_HEADS]


def _attn_in_proj(x, w_qkvz, w_f, b_f, tm):
    m = x.shape[0]
    row = lambda i: (i, 0)
    const = lambda i: (0, 0)
    wide = pl.BlockSpec((tm, D_MODEL), row)
    return pl.pallas_call(
        _attn_in_kernel,
        grid=(m // tm,),
        in_specs=[wide,
                  pl.BlockSpec((D_MODEL, 4 * D_MODEL), const),
                  pl.BlockSpec((D_MODEL, LANES), const),
                  pl.BlockSpec((1, LANES), const)],
        out_specs=[wide, wide, wide, wide, wide, wide, pl.BlockSpec((tm, N_HEADS), row)],
        out_shape=[jax.ShapeDtypeStruct((m, D_MODEL), BF16),
                   jax.ShapeDtypeStruct((m, D_MODEL), F32),
                   jax.ShapeDtypeStruct((m, D_MODEL), F32),
                   jax.ShapeDtypeStruct((m, D_MODEL), BF16),
                   jax.ShapeDtypeStruct((m, D_MODEL), BF16),
                   jax.ShapeDtypeStruct((m, D_MODEL), F32),
                   jax.ShapeDtypeStruct((m, N_HEADS), F32)],
        compiler_params=_cparams(("arbitrary",)),
        name="attn_in_proj",
    )(x, w_qkvz, w_f, b_f)


def _cumsum_kernel(lf_ref, ct_ref, carry_ref, *, tc):
    @pl.when(pl.program_id(0) == 0)
    def _():
        carry_ref[...] = jnp.zeros_like(carry_ref)

    r = lax.broadcasted_iota(jnp.int32, (tc, tc), 0)
    c = lax.broadcasted_iota(jnp.int32, (tc, tc), 1)
    tri = jnp.where(r >= c, 1.0, 0.0).astype(BF16)
    acc = carry_ref[...]
    for part in _split3(lf_ref[...]):
        acc = acc + jnp.dot(tri, part, preferred_element_type=F32)
    carry_ref[...] = acc[tc - 1:tc, :]
    hr = lax.broadcasted_iota(jnp.int32, (N_HEADS, N_HEADS), 0)
    hc = lax.broadcasted_iota(jnp.int32, (N_HEADS, N_HEADS), 1)
    eye = jnp.where(hr == hc, 1.0, 0.0).astype(BF16)
    out = jnp.zeros((N_HEADS, tc), F32)
    for part in _split3(acc):
        out = out + lax.dot_general(eye, part, NT_DIMS, preferred_element_type=F32)
    ct_ref[...] = out


def _cumsum_t(logf, tc=256):
    l = logf.shape[0]
    return pl.pallas_call(
        functools.partial(_cumsum_kernel, tc=tc),
        grid=(l // tc,),
        in_specs=[pl.BlockSpec((tc, N_HEADS), lambda i: (i, 0))],
        out_specs=pl.BlockSpec((N_HEADS, tc), lambda i: (0, i)),
        out_shape=jax.ShapeDtypeStruct((N_HEADS, l), F32),
        scratch_shapes=[pltpu.VMEM((1, N_HEADS), F32)],
        compiler_params=_cparams(("arbitrary",)),
        name="logf_cumsum",
    )(logf)


def _attn_prompt_kernel(q_ref, k_ref, v_ref, ct_ref, o_ref, *, tq):
    p = pl.program_id(0)
    qi = pl.program_id(1)
    q = q_ref[...]
    lane = lax.broadcasted_iota(jnp.int32, (1, LANES), 1)
    first = lane < HEAD_DIM
    zero = jnp.zeros_like(q)
    qs = (jnp.where(first, q, zero), jnp.where(first, zero, q))
    q0 = pl.multiple_of(qi * tq, tq)
    c0 = [ct_ref[pl.ds(2 * p + e, 1), pl.ds(q0, tq)][:, 0:1] for e in range(2)]
    row = lax.broadcasted_iota(jnp.int32, (tq, tq), 0)
    col = lax.broadcasted_iota(jnp.int32, (tq, tq), 1)
    causal = col <= row

    def block(j, carry, masked):
        k0 = pl.multiple_of(j * tq, tq)
        k = k_ref[pl.ds(k0, tq), :]
        v = v_ref[pl.ds(k0, tq), :]
        out = []
        for e in range(2):
            m, l, acc = carry[e]
            s = lax.dot_general(qs[e], k, NT_DIMS, preferred_element_type=F32)
            s = s + (c0[e] - ct_ref[pl.ds(2 * p + e, 1), pl.ds(k0, tq)])
            if masked:
                s = jnp.where(causal, s, -jnp.inf)
            m_new = jnp.maximum(m, jnp.max(s, axis=1, keepdims=True))
            alpha = jnp.exp(m - m_new)
            pr = jnp.exp(s - m_new)
            l = alpha * l + jnp.sum(pr, axis=1, keepdims=True)
            acc = alpha * acc + jnp.dot(pr.astype(BF16), v, preferred_element_type=F32)
            out.append((m_new, l, acc))
        return tuple(out)

    init = tuple((jnp.full((tq, 1), -jnp.inf, F32), jnp.zeros((tq, 1), F32),
                  jnp.zeros((tq, LANES), F32)) for _ in range(2))
    carry = lax.fori_loop(0, qi, lambda j, c: block(j, c, False), init)
    (_, la, acca), (_, lb, accb) = block(qi, carry, True)
    o_ref[...] = jnp.where(first, acca / la, accb / lb)


def _attn_prompt(q, kb, vb, ct, tq=256):
    l = q.shape[0]
    return pl.pallas_call(
        functools.partial(_attn_prompt_kernel, tq=tq),
        grid=(N_SLABS, l // tq),
        in_specs=[pl.BlockSpec((tq, LANES), lambda p, i: (i, p)),
                  pl.BlockSpec((l, LANES), lambda p, i: (0, p)),
                  pl.BlockSpec((l, LANES), lambda p, i: (0, p)),
                  pl.BlockSpec((N_HEADS, l), lambda p, i: (0, 0))],
        out_specs=pl.BlockSpec((tq, LANES), lambda p, i: (i, p)),
        out_shape=jax.ShapeDtypeStruct((l, D_MODEL), F32),
        compiler_params=_cparams(("arbitrary", "arbitrary")),
        name="attn_prompt",
    )(q, kb, vb, ct)


def _attn_out_kernel(o_ref, z_ref, x_ref, w_ref, g_ref, b_ref, y_ref):
    a = (o_ref[...] * _silu(z_ref[...])).astype(BF16)
    sub = jnp.dot(a, w_ref[...], preferred_element_type=F32)
    y_ref[...] = _layer_norm(ALPHA * x_ref[...] + sub, g_ref[...], b_ref[...])


def _attn_out(o, z, x, w_out, g, b, tm):
    m = x.shape[0]
    wide = pl.BlockSpec((tm, D_MODEL), lambda i: (i, 0))
    const = lambda i: (0, 0)
    return pl.pallas_call(
        _attn_out_kernel,
        grid=(m // tm,),
        in_specs=[wide, wide, wide, pl.BlockSpec((D_MODEL, D_MODEL), const),
                  pl.BlockSpec((1, D_MODEL), const), pl.BlockSpec((1, D_MODEL), const)],
        out_specs=wide,
        out_shape=jax.ShapeDtypeStruct((m, D_MODEL), F32),
        compiler_params=_cparams(("arbitrary",)),
        name="attn_out_ln",
    )(o, z, x, w_out, g, b)


def _s5_in_kernel(x_ref, w_ref, u_ref, z_ref):
    xb = x_ref[...].astype(BF16)
    u_ref[...] = jnp.dot(xb, w_ref[:, 0:D_MODEL], preferred_element_type=F32)
    z_ref[...] = jnp.dot(xb, w_ref[:, D_MODEL:2 * D_MODEL], preferred_element_type=F32)


def _s5_in_proj(x, w_in, tm):
    m = x.shape[0]
    wide = pl.BlockSpec((tm, D_MODEL), lambda i: (i, 0))
    return pl.pallas_call(
        _s5_in_kernel,
        grid=(m // tm,),
        in_specs=[wide, pl.BlockSpec((D_MODEL, 2 * D_MODEL), lambda i: (0, 0))],
        out_specs=[wide, wide],
        out_shape=[jax.ShapeDtypeStruct((m, D_MODEL), F32)] * 2,
        compiler_params=_cparams(("arbitrary",)),
        name="s5_in_proj",
    )(x, w_in)


def _s5_param_kernel(are_ref, aim_ref, ldt_ref, arer_ref, aimr_ref, ldtr_ref, bre_ref, bim_ref,
                     pre_ref, pim_ref, nre_ref, nim_ref, bbre_ref, bbim_ref):
    def abar(are, aim, ldt):
        dt = jnp.exp(ldt)
        mag = jnp.exp(are * dt)
        ang = aim * dt
        return mag * jnp.cos(ang), mag * jnp.sin(ang)

    are, aim = are_ref[...], aim_ref[...]
    ar, ai = abar(are, aim, ldt_ref[...])
    inv = 1.0 / (ar * ar + ai * ai)
    nr, ni = ar * inv, -ai * inv
    pr, pi = jnp.ones_like(ar), jnp.zeros_like(ar)
    qr, qi = pr, pi
    for t in range(S5_CHUNK):
        pre_ref[t] = pr
        pim_ref[t] = pi
        nre_ref[t] = qr
        nim_ref[t] = qi
        pr, pi = pr * ar - pi * ai, pr * ai + pi * ar
        qr, qi = qr * nr - qi * ni, qr * ni + qi * nr

    lr, li = arer_ref[...], aimr_ref[...]
    er, ei = abar(lr, li, ldtr_ref[...])
    linv = 1.0 / (lr * lr + li * li)
    fr = ((er - 1.0) * lr + ei * li) * linv
    fi = (ei * lr - (er - 1.0) * li) * linv
    br, bi = bre_ref[...], bim_ref[...]
    bbre_ref[...] = fr * br - fi * bi
    bbim_ref[...] = fr * bi + fi * br


def _s5_params(a_re, a_im, log_dt, b_re, b_im):
    g, p = a_re.shape
    rep = lambda a: jnp.repeat(a, S5_GROUP, axis=0)
    ldt = jnp.broadcast_to(log_dt[:, None], (g, p))
    b_re_t = b_re.transpose(0, 2, 1).reshape(g * S5_GROUP, p)
    b_im_t = b_im.transpose(0, 2, 1).reshape(g * S5_GROUP, p)
    tab = jax.ShapeDtypeStruct((S5_CHUNK, g, p), F32)
    big = jax.ShapeDtypeStruct((g * S5_GROUP, p), F32)
    return pl.pallas_call(
        _s5_param_kernel,
        out_shape=[tab, tab, tab, tab, big, big],
        name="s5_discretise",
    )(a_re, a_im, ldt, rep(a_re), rep(a_im), rep(ldt), b_re_t, b_im_t)


def _slab_table(t_re, t_im):
    t = t_re.shape[0]
    f = lambda a: a.reshape(t, N_SLABS, SLAB_STATE).transpose(1, 0, 2)
    return jnp.concatenate([f(t_re), f(t_im)], axis=-1)


def _s5_weights(bb_re, bb_im, c_re, c_im):
    eye = jnp.eye(GROUPS_PER_SLAB, dtype=F32)
    shape_b = (N_SLABS, GROUPS_PER_SLAB, S5_GROUP, S5_STATE)

    def in_mat(b):
        return jnp.einsum('kgjp,gh->kgjhp', b.reshape(shape_b), eye).reshape(N_SLABS, LANES, SLAB_STATE)

    def out_mat(c):
        return jnp.einsum('kgip,gh->kgphi', c.reshape(shape_b), eye).reshape(N_SLABS, SLAB_STATE, LANES)

    bbd = jnp.concatenate([in_mat(bb_re), in_mat(bb_im)], axis=-1).astype(BF16)
    cbd = jnp.concatenate([out_mat(c_re), -out_mat(c_im)], axis=1).astype(BF16)
    return bbd, cbd


def _cmul(ar, ai, br, bi):
    return ar * br - ai * bi, ar * bi + ai * br


def _s5_scan_kernel(u_ref, bbd_ref, cbd_ref, np_ref, pp_ref, d_ref, h0_ref, y_ref, hl_ref, hc_ref,
                    *, tr):
    t = S5_CHUNK
    half = SLAB_STATE

    @pl.when(pl.program_id(0) == 0)
    def _():
        hc_ref[...] = h0_ref[...]

    r = lax.broadcasted_iota(jnp.int32, (t, t), 0)
    c = lax.broadcasted_iota(jnp.int32, (t, t), 1)
    tri = jnp.where(r >= c, 1.0, 0.0).astype(BF16)

    def chunk(ci, _):
        r0 = pl.multiple_of(ci * t, t)
        for k in range(N_SLABS):
            cols = slice(k * LANES, (k + 1) * LANES)
            uk = u_ref[pl.ds(r0, t), cols]
            bu = jnp.dot(uk.astype(BF16), bbd_ref[k], preferred_element_type=F32)
            zr, zi = _cmul(np_ref[k, :, 0:half], np_ref[k, :, half:], bu[:, 0:half], bu[:, half:])
            zz = jnp.concatenate([zr, zi], axis=1).astype(BF16)
            s = jnp.dot(tri, zz, preferred_element_type=F32)
            hp = hc_ref[k]
            gr, gi = _cmul(pp_ref[k, 1:2, 0:half], pp_ref[k, 1:2, half:], hp[:, 0:half], hp[:, half:])
            hr, hi = _cmul(pp_ref[k, :, 0:half], pp_ref[k, :, half:], s[:, 0:half] + gr, s[:, half:] + gi)
            hc_ref[k] = jnp.concatenate([hr[t - 1:t], hi[t - 1:t]], axis=1)
            hh = jnp.concatenate([hr, hi], axis=1).astype(BF16)
            y = jnp.dot(hh, cbd_ref[k], preferred_element_type=F32) + d_ref[:, cols] * uk
            y_ref[pl.ds(r0, t), cols] = y
        return 0

    lax.fori_loop(0, tr // t, chunk, 0)
    hl_ref[...] = hc_ref[...]


def _s5_scan(u, bbd, cbd, npow, ppow, d, h0, tr=256):
    l = u.shape[0]
    c3 = lambda i: (0, 0, 0)
    state = pl.BlockSpec((N_SLABS, 1, 2 * SLAB_STATE), c3)
    tabspec = pl.BlockSpec((N_SLABS, S5_CHUNK, 2 * SLAB_STATE), c3)
    return pl.pallas_call(
        functools.partial(_s5_scan_kernel, tr=tr),
        grid=(l // tr,),
        in_specs=[pl.BlockSpec((tr, D_MODEL), lambda i: (i, 0)),
                  pl.BlockSpec((N_SLABS, LANES, 2 * SLAB_STATE), c3),
                  pl.BlockSpec((N_SLABS, 2 * SLAB_STATE, LANES), c3),
                  tabspec, tabspec,
                  pl.BlockSpec((1, D_MODEL), lambda i: (0, 0)),
                  state],
        out_specs=[pl.BlockSpec((tr, D_MODEL), lambda i: (i, 0)), state],
        out_shape=[jax.ShapeDtypeStruct((l, D_MODEL), F32),
                   jax.ShapeDtypeStruct((N_SLABS, 1, 2 * SLAB_STATE), F32)],
        scratch_shapes=[pltpu.VMEM((N_SLABS, 1, 2 * SLAB_STATE), F32)],
        compiler_params=_cparams(("arbitrary",)),
        name="s5_scan_prompt",
    )(u, bbd, cbd, npow, ppow, d, h0)


def _s5_step_kernel(u_ref, bbd_ref, cbd_ref, pp_ref, d_ref, h0_ref, y_ref, hl_ref, *, nb, nt):
    half = SLAB_STATE
    for k in range(N_SLABS):
        cols = slice(k * LANES, (k + 1) * LANES)
        uk = u_ref[:, cols]
        bu = jnp.dot(uk.astype(BF16), bbd_ref[k], preferred_element_type=F32)
        ar, ai = pp_ref[k, 1:2, 0:half], pp_ref[k, 1:2, half:]
        h = h0_ref[k]
        hr, hi = h[:, 0:half], h[:, half:]
        hs = []
        for t in range(nt):
            rows = slice(t * nb, (t + 1) * nb)
            gr, gi = _cmul(ar, ai, hr, hi)
            hr, hi = gr + bu[rows, 0:half], gi + bu[rows, half:]
            hs.append(jnp.concatenate([hr, hi], axis=1))
        hl_ref[k] = hs[-1]
        hh = jnp.concatenate(hs, axis=0).astype(BF16)
        y_ref[:, cols] = jnp.dot(hh, cbd_ref[k], preferred_element_type=F32) + d_ref[:, cols] * uk


def _s5_steps(u_tm, bbd, cbd, ppow, d, h0, nb, nt):
    return pl.pallas_call(
        functools.partial(_s5_step_kernel, nb=nb, nt=nt),
        out_shape=[jax.ShapeDtypeStruct((nt * nb, D_MODEL), F32),
                   jax.ShapeDtypeStruct((N_SLABS, nb, 2 * SLAB_STATE), F32)],
        compiler_params=pltpu.CompilerParams(vmem_limit_bytes=VMEM_LIMIT),
        name="s5_scan_sample",
    )(u_tm, bbd, cbd, ppow, d, h0)


def _s5_out_kernel(y_ref, z_ref, x_ref, wg_ref, bg_ref, wo_ref, g_ref, b_ref, o_ref):
    g = jax.nn.gelu(y_ref[...])
    gate = jax.nn.sigmoid(jnp.dot(g.astype(BF16), wg_ref[...], preferred_element_type=F32) + bg_ref[...])
    a = (g * gate * _silu(z_ref[...])).astype(BF16)
    sub = jnp.dot(a, wo_ref[...], preferred_element_type=F32)
    o_ref[...] = _layer_norm(ALPHA * x_ref[...] + sub, g_ref[...], b_ref[...])


def _s5_out(y, z, x, w_glu, b_glu, w_out, g, b, tm):
    m = x.shape[0]
    wide = pl.BlockSpec((tm, D_MODEL), lambda i: (i, 0))
    const = lambda i: (0, 0)
    mat = pl.BlockSpec((D_MODEL, D_MODEL), const)
    vec = pl.BlockSpec((1, D_MODEL), const)
    return pl.pallas_call(
        _s5_out_kernel,
        grid=(m // tm,),
        in_specs=[wide, wide, wide, mat, vec, mat, vec, vec],
        out_specs=wide,
        out_shape=jax.ShapeDtypeStruct((m, D_MODEL), F32),
        compiler_params=_cparams(("arbitrary",)),
        name="s5_out_ln",
    )(y, z, x, w_glu, b_glu, w_out, g, b)


def _past_bias_kernel(pt_ref, lf_hbm, y_ref, buf_ref, sem, *, n_pages):
    b = pl.program_id(0)
    width = PAGE * N_HEADS

    def page_copy(i):
        return pltpu.make_async_copy(lf_hbm.at[pl.ds(pt_ref[b * n_pages + i], 1)],
                                     buf_ref.at[pl.ds(i, 1)], sem)

    def start(i, _):
        page_copy(i).start()
        return 0

    def wait(i, _):
        page_copy(i).wait()
        return 0

    lax.fori_loop(0, n_pages, start, 0)
    lax.fori_loop(0, n_pages, wait, 0)

    x = buf_ref[...]
    lane = lax.broadcasted_iota(jnp.int32, (1, width), 1)
    incl = x
    d = N_HEADS
    while d < width:
        shifted = pltpu.roll(incl, width - d, axis=1)
        incl = incl + jnp.where(lane < width - d, shifted, 0.0)
        d *= 2
    excl = jnp.where(lane < width - N_HEADS, pltpu.roll(incl, width - N_HEADS, axis=1), 0.0)
    tot = incl[:, 0:N_HEADS]
    r = lax.broadcasted_iota(jnp.int32, (n_pages, n_pages), 0)
    c = lax.broadcasted_iota(jnp.int32, (n_pages, n_pages), 1)
    later = jnp.where(c > r, 1.0, 0.0).astype(BF16)
    er = lax.broadcasted_iota(jnp.int32, (N_HEADS, width), 0)
    ec = lax.broadcasted_iota(jnp.int32, (N_HEADS, width), 1)
    expand = jnp.where((ec & (N_HEADS - 1)) == er, 1.0, 0.0).astype(BF16)
    after = jnp.zeros((n_pages, N_HEADS), F32)
    for part in _split3(tot):
        after = after + jnp.dot(later, part, preferred_element_type=F32)
    out = excl
    for part in _split3(after):
        out = out + jnp.dot(part, expand, preferred_element_type=F32)
    y_ref[0] = out


def _past_bias(page_table, cache_logf_l):
    nb, n_pages = page_table.shape
    width = PAGE * N_HEADS
    lf2 = cache_logf_l.reshape(cache_logf_l.shape[0], width)
    y = pl.pallas_call(
        functools.partial(_past_bias_kernel, n_pages=n_pages),
        grid_spec=pltpu.PrefetchScalarGridSpec(
            num_scalar_prefetch=1,
            grid=(nb,),
            in_specs=[pl.BlockSpec(memory_space=pl.ANY)],
            out_specs=pl.BlockSpec((1, n_pages, width), lambda b, pt: (b, 0, 0)),
            scratch_shapes=[pltpu.VMEM((n_pages, width), F32), pltpu.SemaphoreType.DMA(())]),
        out_shape=jax.ShapeDtypeStruct((nb, n_pages, width), F32),
        compiler_params=_cparams(("arbitrary",)),
        name="past_bias",
    )(page_table.reshape(-1), lf2)
    return y.reshape(nb, n_pages, PAGE, N_HEADS)


def _decode_kernel(pt_ref, q_ref, kn_ref, vn_ref, lfn_ref, kc_ref, vc_ref, yb_ref, o_ref,
                   qbd_ref, m_ref, l_ref, acc_ref, *, nt, n_pages):
    j = pl.program_id(1)
    rows = nt * N_HEADS
    rsel = lax.broadcasted_iota(jnp.int32, (rows, N_HEADS), 0)
    csel = lax.broadcasted_iota(jnp.int32, (rows, N_HEADS), 1)
    sel = jnp.where((rsel & (N_HEADS - 1)) == csel, 1.0, 0.0).astype(BF16)

    def scores(kb, ybias):
        s = lax.dot_general(qbd_ref[...], kb, NT_DIMS, preferred_element_type=F32)
        for part in _split2(ybias):
            s = s + lax.dot_general(sel, part, NT_DIMS, preferred_element_type=F32)
        return s

    @pl.when(j == 0)
    def _():
        hrow = lax.broadcasted_iota(jnp.int32, (N_HEADS, D_MODEL), 0)
        hcol = lax.broadcasted_iota(jnp.int32, (N_HEADS, D_MODEL), 1)
        own = (hcol // HEAD_DIM) == hrow
        q = q_ref[0].astype(F32)
        qbd = [jnp.where(own, jnp.broadcast_to(q[t:t + 1], (N_HEADS, D_MODEL)), 0.0) for t in range(nt)]
        qbd_ref[...] = jnp.concatenate(qbd, axis=0).astype(BF16)

        key_row = lax.broadcasted_iota(jnp.int32, (PAGE, D_MODEL), 0)
        key_row_h = lax.broadcasted_iota(jnp.int32, (PAGE, N_HEADS), 0)
        kn, vn, lfn = kn_ref[0], vn_ref[0], lfn_ref[0]
        kpad = jnp.zeros((PAGE, D_MODEL), F32)
        vpad = jnp.zeros((PAGE, D_MODEL), F32)
        ypad = jnp.zeros((PAGE, N_HEADS), F32)
        cn = jnp.zeros((1, N_HEADS), F32)
        for t in range(nt):
            cn = cn + lfn[t:t + 1]
            kpad = jnp.where(key_row == t, jnp.broadcast_to(kn[t:t + 1], (PAGE, D_MODEL)), kpad)
            vpad = jnp.where(key_row == t, jnp.broadcast_to(vn[t:t + 1], (PAGE, D_MODEL)), vpad)
            ypad = jnp.where(key_row_h == t, jnp.broadcast_to(-cn, (PAGE, N_HEADS)), ypad)
        s = scores(kpad.astype(BF16), ypad)
        srow = lax.broadcasted_iota(jnp.int32, (rows, PAGE), 0)
        scol = lax.broadcasted_iota(jnp.int32, (rows, PAGE), 1)
        s = jnp.where(scol <= (srow // N_HEADS), s, -jnp.inf)
        m = jnp.max(s, axis=1, keepdims=True)
        pr = jnp.exp(s - m)
        m_ref[...] = m
        l_ref[...] = jnp.sum(pr, axis=1, keepdims=True)
        acc_ref[...] = jnp.dot(pr.astype(BF16), vpad.astype(BF16), preferred_element_type=F32)

    @pl.when(j > 0)
    def _():
        s = scores(kc_ref[0].astype(BF16), yb_ref[0, 0])
        m = m_ref[...]
        m_new = jnp.maximum(m, jnp.max(s, axis=1, keepdims=True))
        corr = jnp.exp(m - m_new)
        pr = jnp.exp(s - m_new)
        m_ref[...] = m_new
        l_ref[...] = l_ref[...] * corr + jnp.sum(pr, axis=1, keepdims=True)
        acc_ref[...] = acc_ref[...] * corr + jnp.dot(pr.astype(BF16), vc_ref[0].astype(BF16),
                                                     preferred_element_type=F32)

    @pl.when(j == n_pages)
    def _():
        hrow = lax.broadcasted_iota(jnp.int32, (N_HEADS, D_MODEL), 0)
        hcol = lax.broadcasted_iota(jnp.int32, (N_HEADS, D_MODEL), 1)
        own = (hcol // HEAD_DIM) == hrow
        o = acc_ref[...] / l_ref[...]
        for t in range(nt):
            blk = jnp.where(own, o[t * N_HEADS:(t + 1) * N_HEADS], 0.0)
            o_ref[0, t:t + 1, :] = jnp.sum(blk, axis=0, keepdims=True)


def _decode_attn(page_table, q, k_new, v_new, lf_new, cache_k_l, cache_v_l, ybias):
    nb, n_pages = page_table.shape
    nt = q.shape[1]
    rows = nt * N_HEADS
    n_pool = cache_k_l.shape[0]
    kc = cache_k_l.reshape(n_pool, PAGE, D_MODEL)
    vc = cache_v_l.reshape(n_pool, PAGE, D_MODEL)

    def logical(j):
        return n_pages - jnp.maximum(j, 1)

    per_b = lambda b, j, pt: (b, 0, 0)
    page = lambda b, j, pt: (pt[b * n_pages + logical(j)], 0, 0)
    return pl.pallas_call(
        functools.partial(_decode_kernel, nt=nt, n_pages=n_pages),
        grid_spec=pltpu.PrefetchScalarGridSpec(
            num_scalar_prefetch=1,
            grid=(nb, n_pages + 1),
            in_specs=[pl.BlockSpec((1, nt, D_MODEL), per_b),
                      pl.BlockSpec((1, nt, D_MODEL), per_b),
                      pl.BlockSpec((1, nt, D_MODEL), per_b),
                      pl.BlockSpec((1, nt, N_HEADS), per_b),
                      pl.BlockSpec((1, PAGE, D_MODEL), page),
                      pl.BlockSpec((1, PAGE, D_MODEL), page),
                      pl.BlockSpec((1, 1, PAGE, N_HEADS), lambda b, j, pt: (b, logical(j), 0, 0))],
            out_specs=pl.BlockSpec((1, nt, D_MODEL), per_b),
            scratch_shapes=[pltpu.VMEM((rows, D_MODEL), BF16),
                            pltpu.VMEM((rows, 1), F32),
                            pltpu.VMEM((rows, 1), F32),
                            pltpu.VMEM((rows, D_MODEL), F32)]),
        out_shape=jax.ShapeDtypeStruct((nb, nt, D_MODEL), F32),
        compiler_params=_cparams(("arbitrary", "arbitrary")),
        name="decode_attn",
    )(page_table.reshape(-1), q, k_new, v_new, lf_new, kc, vc, ybias)


def _row_tile(m):
    return 512 if m % 512 == 0 else m


def kernel(x_prompt, x_sample, cache_k, cache_v, cache_logf, page_table, state_s5_re, state_s5_im,
           attn_w_in, attn_b_f, attn_w_out, s5_w_in, s5_a_re, s5_a_im, s5_log_dt,
           s5_b_re, s5_b_im, s5_c_re, s5_c_im, s5_d, s5_w_glu, s5_b_glu, s5_w_out,
           ln_g, ln_b):
    assert x_prompt.shape[0] == 1
    l = x_prompt.shape[1]
    nb, nt = x_sample.shape[:2]
    width = N_HEADS * HEAD_DIM

    w_in = attn_w_in[0]
    w_qkvz = w_in[:, :4 * width].astype(BF16)
    w_f = jnp.pad(w_in[:, 4 * width:], ((0, 0), (0, LANES - N_HEADS))).astype(BF16)
    b_f = jnp.pad(attn_b_f[0], (0, LANES - N_HEADS)).reshape(1, LANES)
    w_ao = attn_w_out[0].astype(BF16)
    w_s5in = s5_w_in[0].astype(BF16)
    w_glu = s5_w_glu[0].astype(BF16)
    b_glu = s5_b_glu[0].reshape(1, D_MODEL)
    w_so = s5_w_out[0].astype(BF16)
    g0, b0 = ln_g[0].reshape(1, D_MODEL), ln_b[0].reshape(1, D_MODEL)
    g1, b1 = ln_g[1].reshape(1, D_MODEL), ln_b[1].reshape(1, D_MODEL)
    d_skip = s5_d[0].reshape(1, D_MODEL)

    pre, pim, nre, nim, bb_re, bb_im = _s5_params(s5_a_re[0], s5_a_im[0], s5_log_dt[0],
                                                  s5_b_re[0], s5_b_im[0])
    ppow = _slab_table(pre, pim)
    npow = _slab_table(nre, nim)
    bbd, cbd = _s5_weights(bb_re, bb_im, s5_c_re[0], s5_c_im[0])

    def to_slab_state(re, im):
        f = lambda a: a.reshape(a.shape[0], N_SLABS, SLAB_STATE).transpose(1, 0, 2)
        return jnp.concatenate([f(re), f(im)], axis=-1)

    def from_slab_state(h):
        f = lambda a: a.transpose(1, 0, 2).reshape(a.shape[1], N_SLABS * GROUPS_PER_SLAB, S5_STATE)
        return f(h[..., :SLAB_STATE]), f(h[..., SLAB_STATE:])

    xp = x_prompt[0]
    tm = _row_tile(l)
    q, k, v, kb, vb, z, lf = _attn_in_proj(xp, w_qkvz, w_f, b_f, tm)
    ct = _cumsum_t(lf)
    o = _attn_prompt(q, kb, vb, ct)
    x1 = _attn_out(o, z, xp, w_ao, g0, b0, tm)
    u, z1 = _s5_in_proj(x1, w_s5in, tm)
    zero_state = jnp.zeros((N_SLABS, 1, 2 * SLAB_STATE), F32)
    y, h_last = _s5_scan(u, bbd, cbd, npow, ppow, d_skip, zero_state)
    y_prompt = _s5_out(y, z1, x1, w_glu, b_glu, w_so, g1, b1, tm)
    hp_re, hp_im = from_slab_state(h_last)

    ms = nb * nt
    xs = x_sample.reshape(ms, D_MODEL)
    qs, ks, vs, _, _, zs, lfs = _attn_in_proj(xs, w_qkvz, w_f, b_f, ms)
    ybias = _past_bias(page_table, cache_logf[0])
    os_ = _decode_attn(page_table, qs.reshape(nb, nt, D_MODEL), ks.reshape(nb, nt, D_MODEL),
                       vs.reshape(nb, nt, D_MODEL), lfs.reshape(nb, nt, N_HEADS),
                       cache_k[0], cache_v[0], ybias)
    xs1 = _attn_out(os_.reshape(ms, D_MODEL), zs, xs, w_ao, g0, b0, ms)
    us, zs1 = _s5_in_proj(xs1, w_s5in, ms)
    tmaj = lambda a: a.reshape(nb, nt, D_MODEL).transpose(1, 0, 2).reshape(ms, D_MODEL)
    bmaj = lambda a: a.reshape(nt, nb, D_MODEL).transpose(1, 0, 2).reshape(ms, D_MODEL)
    h0s = to_slab_state(state_s5_re[0], state_s5_im[0])
    ys_tm, hs_last = _s5_steps(tmaj(us), bbd, cbd, ppow, d_skip, h0s, nb, nt)
    y_sample = _s5_out(bmaj(ys_tm), zs1, xs1, w_glu, b_glu, w_so, g1, b1, ms)
    hs_re, hs_im = from_slab_state(hs_last)

    kv5 = lambda a, bsz, t: a.reshape(1, bsz, t, N_HEADS, HEAD_DIM)
    return (y_prompt.reshape(1, l, D_MODEL),
            y_sample.reshape(nb, nt, D_MODEL),
            kv5(k, 1, l), kv5(v, 1, l), lf.reshape(1, 1, l, N_HEADS),
            kv5(ks, nb, nt), kv5(vs, nb, nt), lfs.reshape(1, nb, nt, N_HEADS),
            hp_re[None], hp_im[None], hs_re[None], hs_im[None])
```

```python
import functools
import math

import jax
import jax.numpy as jnp
from jax import lax
from jax.experimental import pallas as pl
from jax.experimental.pallas import tpu as pltpu

F32 = jnp.float32
BF16 = jnp.bfloat16

D_MODEL = 1024
N_HEADS = 16
HEAD_DIM = 64
PAGE = 128
DEPTH = 2
ALPHA = (2 * DEPTH) ** 0.25
LN_EPS = 1e-5
QK_SCALE = HEAD_DIM ** -0.5
LOG2E = math.log2(math.e)

LANES = 128
N_SLABS = D_MODEL // LANES
S5_GROUP = 16
S5_STATE = 64
GROUPS_PER_SLAB = LANES // S5_GROUP
SLAB_STATE = GROUPS_PER_SLAB * S5_STATE
S5_CHUNK = 64
VMEM_LIMIT = 56 * 1024 * 1024

NT_DIMS = (((1,), (1,)), ((), ()))


def _cparams(sem):
    return pltpu.CompilerParams(dimension_semantics=sem, vmem_limit_bytes=VMEM_LIMIT)


def _split2(x):
    hi = x.astype(BF16)
    lo = (x - hi.astype(F32)).astype(BF16)
    return hi, lo


def _split3(x):
    hi = x.astype(BF16)
    r = x - hi.astype(F32)
    mid = r.astype(BF16)
    lo = (r - mid.astype(F32)).astype(BF16)
    return hi, mid, lo


def _layer_norm(r, g, b):
    mu = jnp.mean(r, axis=-1, keepdims=True)
    c = r - mu
    var = jnp.mean(c * c, axis=-1, keepdims=True)
    return c * lax.rsqrt(var + LN_EPS) * g + b


def _silu(z):
    return z * jax.nn.sigmoid(z)


def _attn_in_kernel(x_ref, w_ref, wf_ref, bf_ref, q_ref, k_ref, v_ref, kb_ref, vb_ref, z_ref, lf_ref):
    xb = x_ref[...].astype(BF16)
    q = jnp.dot(xb, w_ref[:, 0:D_MODEL], preferred_element_type=F32)
    q_ref[...] = (q * QK_SCALE).astype(BF16)
    k = jnp.dot(xb, w_ref[:, D_MODEL:2 * D_MODEL], preferred_element_type=F32)
    k_ref[...] = k
    kb_ref[...] = k.astype(BF16)
    v = jnp.dot(xb, w_ref[:, 2 * D_MODEL:3 * D_MODEL], preferred_element_type=F32)
    v_ref[...] = v
    vb_ref[...] = v.astype(BF16)
    z_ref[...] = jnp.dot(xb, w_ref[:, 3 * D_MODEL:4 * D_MODEL], preferred_element_type=F32)
    hf = jnp.dot(xb, wf_ref[...], preferred_element_type=F32) + bf_ref[...]
    lf = jnp.minimum(hf, 0.0) - jnp.log(1.0 + jnp.exp(-jnp.abs(hf)))
    lf_ref[...] = lf[:, 0:N_HEADS]


def _attn_in_proj(x, w_qkvz, w_f, b_f, tm):
    m = x.shape[0]
    row = lambda i: (i, 0)
    const = lambda i: (0, 0)
    wide = pl.BlockSpec((tm, D_MODEL), row)
    return pl.pallas_call(
        _attn_in_kernel,
        grid=(m // tm,),
        in_specs=[wide,
                  pl.BlockSpec((D_MODEL, 4 * D_MODEL), const),
                  pl.BlockSpec((D_MODEL, LANES), const),
                  pl.BlockSpec((1, LANES), const)],
        out_specs=[wide, wide, wide, wide, wide, wide, pl.BlockSpec((tm, N_HEADS), row)],
        out_shape=[jax.ShapeDtypeStruct((m, D_MODEL), BF16),
                   jax.ShapeDtypeStruct((m, D_MODEL), F32),
                   jax.ShapeDtypeStruct((m, D_MODEL), F32),
                   jax.ShapeDtypeStruct((m, D_MODEL), BF16),
                   jax.ShapeDtypeStruct((m, D_MODEL), BF16),
                   jax.ShapeDtypeStruct((m, D_MODEL), F32),
                   jax.ShapeDtypeStruct((m, N_HEADS), F32)],
        compiler_params=_cparams(("arbitrary",)),
        name="attn_in_proj",
    )(x, w_qkvz, w_f, b_f)


def _attn_in_prompt_kernel(x_ref, w_ref, wt_ref, wf_ref, bf_ref,
                           q_ref, vb_ref, z_ref, kt_ref, ktb_ref, vt_ref, lf_ref):
    xb = x_ref[...].astype(BF16)
    q = jnp.dot(xb, w_ref[:, 0:D_MODEL], preferred_element_type=F32)
    q_ref[...] = (q * (QK_SCALE * LOG2E)).astype(BF16)
    vb_ref[...] = jnp.dot(xb, w_ref[:, D_MODEL:2 * D_MODEL], preferred_element_type=F32).astype(BF16)
    z_ref[...] = jnp.dot(xb, w_ref[:, 2 * D_MODEL:3 * D_MODEL], preferred_element_type=F32)
    kt = lax.dot_general(wt_ref[0:D_MODEL, :], xb, NT_DIMS, preferred_element_type=F32)
    kt_ref[...] = kt
    ktb_ref[...] = kt.astype(BF16)
    vt_ref[...] = lax.dot_general(wt_ref[D_MODEL:2 * D_MODEL, :], xb, NT_DIMS, preferred_element_type=F32)
    hf = jnp.dot(xb, wf_ref[...], preferred_element_type=F32) + bf_ref[...]
    lf = jnp.minimum(hf, 0.0) - jnp.log(1.0 + jnp.exp(-jnp.abs(hf)))
    lf_ref[...] = lf[:, 0:N_HEADS]


def _attn_in_proj_prompt(x, w_qvz, wt_kv, w_f, b_f, tm):
    m = x.shape[0]
    row = lambda i: (i, 0)
    col = lambda i: (0, i)
    const = lambda i: (0, 0)
    wide = pl.BlockSpec((tm, D_MODEL), row)
    tall = pl.BlockSpec((D_MODEL, tm), col)
    return pl.pallas_call(
        _attn_in_prompt_kernel,
        grid=(m // tm,),
        in_specs=[wide,
                  pl.BlockSpec((D_MODEL, 3 * D_MODEL), const),
                  pl.BlockSpec((2 * D_MODEL, D_MODEL), const),
                  pl.BlockSpec((D_MODEL, LANES), const),
                  pl.BlockSpec((1, LANES), const)],
        out_specs=[wide, wide, wide, tall, tall, tall, pl.BlockSpec((tm, N_HEADS), row)],
        out_shape=[jax.ShapeDtypeStruct((m, D_MODEL), BF16),
                   jax.ShapeDtypeStruct((m, D_MODEL), BF16),
                   jax.ShapeDtypeStruct((m, D_MODEL), F32),
                   jax.ShapeDtypeStruct((D_MODEL, m), F32),
                   jax.ShapeDtypeStruct((D_MODEL, m), BF16),
                   jax.ShapeDtypeStruct((D_MODEL, m), F32),
                   jax.ShapeDtypeStruct((m, N_HEADS), F32)],
        compiler_params=_cparams(("arbitrary",)),
        name="attn_in_proj_prompt",
    )(x, w_qvz, wt_kv, w_f, b_f)


def _cumsum_kernel(lf_ref, ct_ref, carry_ref, *, tc):
    @pl.when(pl.program_id(0) == 0)
    def _():
        carry_ref[...] = jnp.zeros_like(carry_ref)

    r = lax.broadcasted_iota(jnp.int32, (tc, tc), 0)
    c = lax.broadcasted_iota(jnp.int32, (tc, tc), 1)
    tri = jnp.where(r >= c, 1.0, 0.0).astype(BF16)
    acc = carry_ref[...]
    for part in _split3(lf_ref[...]):
        acc = acc + jnp.dot(tri, part, preferred_element_type=F32)
    carry_ref[...] = acc[tc - 1:tc, :]
    hr = lax.broadcasted_iota(jnp.int32, (N_HEADS, N_HEADS), 0)
    hc = lax.broadcasted_iota(jnp.int32, (N_HEADS, N_HEADS), 1)
    eye = jnp.where(hr == hc, 1.0, 0.0).astype(BF16)
    out = jnp.zeros((N_HEADS, tc), F32)
    for part in _split3(acc):
        out = out + lax.dot_general(eye, part, NT_DIMS, preferred_element_type=F32)
    ct_ref[...] = out


def _cumsum_t(logf, tc=256):
    l = logf.shape[0]
    return pl.pallas_call(
        functools.partial(_cumsum_kernel, tc=tc),
        grid=(l // tc,),
        in_specs=[pl.BlockSpec((tc, N_HEADS), lambda i: (i, 0))],
        out_specs=pl.BlockSpec((N_HEADS, tc), lambda i: (0, i)),
        out_shape=jax.ShapeDtypeStruct((N_HEADS, l), F32),
        scratch_shapes=[pltpu.VMEM((1, N_HEADS), F32)],
        compiler_params=_cparams(("arbitrary",)),
        name="logf_cumsum",
    )(logf)


def _attn_prompt_kernel(q_ref, kt_ref, v_ref, ct_ref, o_ref, *, tq):
    p = pl.program_id(0)
    qi = pl.program_id(1)
    q = q_ref[...]
    lane = lax.broadcasted_iota(jnp.int32, (1, LANES), 1)
    first = lane < HEAD_DIM
    zero = jnp.zeros_like(q)
    qs = (jnp.where(first, q, zero), jnp.where(first, zero, q))
    ones_col = (jnp.where(lane == HEAD_DIM, 1.0, 0.0).astype(BF16), jnp.where(lane == 0, 1.0, 0.0).astype(BF16))
    q0 = pl.multiple_of(qi * tq, tq)
    c0 = [ct_ref[pl.ds(2 * p + e, 1), pl.ds(q0, tq)][:, 0:1] for e in range(2)]
    row = lax.broadcasted_iota(jnp.int32, (tq, tq), 0)
    col = lax.broadcasted_iota(jnp.int32, (tq, tq), 1)
    causal = col <= row

    def block(j, carry, masked):
        k0 = pl.multiple_of(j * tq, tq)
        kt = kt_ref[:, pl.ds(k0, tq)]
        v = v_ref[pl.ds(k0, tq), :]
        vs = (jnp.where(first, v, ones_col[0]), jnp.where(first, ones_col[1], v))
        out = []
        for e in range(2):
            m, acc = carry[e]
            bias = (c0[e] - ct_ref[pl.ds(2 * p + e, 1), pl.ds(k0, tq)]) * LOG2E
            s = jnp.dot(qs[e], kt, preferred_element_type=F32) + bias
            if masked:
                s = jnp.where(causal, s, -jnp.inf)
            m_new = jnp.maximum(m, jnp.max(s, axis=1, keepdims=True))
            pr = jnp.exp2(s - m_new).astype(BF16)
            acc = jnp.exp2(m - m_new) * acc + jnp.dot(pr, vs[e], preferred_element_type=F32)
            out.append((m_new, acc))
        return tuple(out)

    init = tuple((jnp.full((tq, 1), -jnp.inf, F32), jnp.zeros((tq, LANES), F32)) for _ in range(2))
    carry = lax.fori_loop(0, qi, lambda j, c: block(j, c, False), init)
    (_, acca), (_, accb) = block(qi, carry, True)
    o_ref[...] = jnp.where(first, acca / acca[:, HEAD_DIM:HEAD_DIM + 1], accb / accb[:, 0:1])


def _attn_prompt(q, ktb, vb, ct, tq=512):
    l = q.shape[0]
    return pl.pallas_call(
        functools.partial(_attn_prompt_kernel, tq=tq),
        grid=(N_SLABS, l // tq),
        in_specs=[pl.BlockSpec((tq, LANES), lambda p, i: (i, p)),
                  pl.BlockSpec((LANES, l), lambda p, i: (p, 0)),
                  pl.BlockSpec((l, LANES), lambda p, i: (0, p)),
                  pl.BlockSpec((N_HEADS, l), lambda p, i: (0, 0))],
        out_specs=pl.BlockSpec((tq, LANES), lambda p, i: (i, p)),
        out_shape=jax.ShapeDtypeStruct((l, D_MODEL), F32),
        compiler_params=_cparams(("arbitrary", "arbitrary")),
        name="attn_prompt",
    )(q, ktb, vb, ct)


def _attn_out_kernel(o_ref, z_ref, x_ref, w_ref, g_ref, b_ref, y_ref):
    a = (o_ref[...] * _silu(z_ref[...])).astype(BF16)
    sub = jnp.dot(a, w_ref[...], preferred_element_type=F32)
    y_ref[...] = _layer_norm(ALPHA * x_ref[...] + sub, g_ref[...], b_ref[...])


def _attn_out(o, z, x, w_out, g, b, tm):
    m = x.shape[0]
    wide = pl.BlockSpec((tm, D_MODEL), lambda i: (i, 0))
    const = lambda i: (0, 0)
    return pl.pallas_call(
        _attn_out_kernel,
        grid=(m // tm,),
        in_specs=[wide, wide, wide, pl.BlockSpec((D_MODEL, D_MODEL), const),
                  pl.BlockSpec((1, D_MODEL), const), pl.BlockSpec((1, D_MODEL), const)],
        out_specs=wide,
        out_shape=jax.ShapeDtypeStruct((m, D_MODEL), F32),
        compiler_params=_cparams(("arbitrary",)),
        name="attn_out_ln",
    )(o, z, x, w_out, g, b)


def _s5_in_kernel(x_ref, w_ref, u_ref, z_ref):
    xb = x_ref[...].astype(BF16)
    u_ref[...] = jnp.dot(xb, w_ref[:, 0:D_MODEL], preferred_element_type=F32)
    z_ref[...] = jnp.dot(xb, w_ref[:, D_MODEL:2 * D_MODEL], preferred_element_type=F32)


def _s5_in_proj(x, w_in, tm):
    m = x.shape[0]
    wide = pl.BlockSpec((tm, D_MODEL), lambda i: (i, 0))
    return pl.pallas_call(
        _s5_in_kernel,
        grid=(m // tm,),
        in_specs=[wide, pl.BlockSpec((D_MODEL, 2 * D_MODEL), lambda i: (0, 0))],
        out_specs=[wide, wide],
        out_shape=[jax.ShapeDtypeStruct((m, D_MODEL), F32)] * 2,
        compiler_params=_cparams(("arbitrary",)),
        name="s5_in_proj",
    )(x, w_in)


def _s5_param_kernel(are_ref, aim_ref, ldt_ref, arer_ref, aimr_ref, ldtr_ref, bre_ref, bim_ref,
                     pre_ref, pim_ref, nre_ref, nim_ref, bbre_ref, bbim_ref):
    def abar(are, aim, ldt):
        dt = jnp.exp(ldt)
        mag = jnp.exp(are * dt)
        ang = aim * dt
        return mag * jnp.cos(ang), mag * jnp.sin(ang)

    are, aim = are_ref[...], aim_ref[...]
    ar, ai = abar(are, aim, ldt_ref[...])
    inv = 1.0 / (ar * ar + ai * ai)
    nr, ni = ar * inv, -ai * inv
    pr, pi = jnp.ones_like(ar), jnp.zeros_like(ar)
    qr, qi = pr, pi
    for t in range(S5_CHUNK):
        pre_ref[t] = pr
        pim_ref[t] = pi
        nre_ref[t] = qr
        nim_ref[t] = qi
        pr, pi = pr * ar - pi * ai, pr * ai + pi * ar
        qr, qi = qr * nr - qi * ni, qr * ni + qi * nr

    lr, li = arer_ref[...], aimr_ref[...]
    er, ei = abar(lr, li, ldtr_ref[...])
    linv = 1.0 / (lr * lr + li * li)
    fr = ((er - 1.0) * lr + ei * li) * linv
    fi = (ei * lr - (er - 1.0) * li) * linv
    br, bi = bre_ref[...], bim_ref[...]
    bbre_ref[...] = fr * br - fi * bi
    bbim_ref[...] = fr * bi + fi * br


def _s5_params(a_re, a_im, log_dt, b_re, b_im):
    g, p = a_re.shape
    rep = lambda a: jnp.repeat(a, S5_GROUP, axis=0)
    ldt = jnp.broadcast_to(log_dt[:, None], (g, p))
    b_re_t = b_re.transpose(0, 2, 1).reshape(g * S5_GROUP, p)
    b_im_t = b_im.transpose(0, 2, 1).reshape(g * S5_GROUP, p)
    tab = jax.ShapeDtypeStruct((S5_CHUNK, g, p), F32)
    big = jax.ShapeDtypeStruct((g * S5_GROUP, p), F32)
    return pl.pallas_call(
        _s5_param_kernel,
        out_shape=[tab, tab, tab, tab, big, big],
        name="s5_discretise",
    )(a_re, a_im, ldt, rep(a_re), rep(a_im), rep(ldt), b_re_t, b_im_t)


def _slab_table(t_re, t_im):
    t = t_re.shape[0]
    f = lambda a: a.reshape(t, N_SLABS, SLAB_STATE).transpose(1, 0, 2)
    return jnp.concatenate([f(t_re), f(t_im)], axis=-1)


def _s5_weights(bb_re, bb_im, c_re, c_im):
    eye = jnp.eye(GROUPS_PER_SLAB, dtype=F32)
    shape_b = (N_SLABS, GROUPS_PER_SLAB, S5_GROUP, S5_STATE)

    def in_mat(b):
        return jnp.einsum('kgjp,gh->kgjhp', b.reshape(shape_b), eye).reshape(N_SLABS, LANES, SLAB_STATE)

    def out_mat(c):
        return jnp.einsum('kgip,gh->kgphi', c.reshape(shape_b), eye).reshape(N_SLABS, SLAB_STATE, LANES)

    bbd = jnp.concatenate([in_mat(bb_re), in_mat(bb_im)], axis=-1).astype(BF16)
    cbd = jnp.concatenate([out_mat(c_re), -out_mat(c_im)], axis=1).astype(BF16)
    return bbd, cbd


def _cmul(ar, ai, br, bi):
    return ar * br - ai * bi, ar * bi + ai * br


def _s5_scan_kernel(u_ref, bbd_ref, cbd_ref, np_ref, pp_ref, d_ref, h0_ref, y_ref, hl_ref, hc_ref,
                    *, tr):
    t = S5_CHUNK
    half = SLAB_STATE

    @pl.when(pl.program_id(0) == 0)
    def _():
        hc_ref[...] = h0_ref[...]

    r = lax.broadcasted_iota(jnp.int32, (t, t), 0)
    c = lax.broadcasted_iota(jnp.int32, (t, t), 1)
    tri = jnp.where(r >= c, 1.0, 0.0).astype(BF16)

    def chunk(ci, _):
        r0 = pl.multiple_of(ci * t, t)
        for k in range(N_SLABS):
            cols = slice(k * LANES, (k + 1) * LANES)
            uk = u_ref[pl.ds(r0, t), cols]
            bu = jnp.dot(uk.astype(BF16), bbd_ref[k], preferred_element_type=F32)
            zr, zi = _cmul(np_ref[k, :, 0:half], np_ref[k, :, half:], bu[:, 0:half], bu[:, half:])
            zz = jnp.concatenate([zr, zi], axis=1).astype(BF16)
            s = jnp.dot(tri, zz, preferred_element_type=F32)
            hp = hc_ref[k]
            gr, gi = _cmul(pp_ref[k, 1:2, 0:half], pp_ref[k, 1:2, half:], hp[:, 0:half], hp[:, half:])
            hr, hi = _cmul(pp_ref[k, :, 0:half], pp_ref[k, :, half:], s[:, 0:half] + gr, s[:, half:] + gi)
            hc_ref[k] = jnp.concatenate([hr[t - 1:t], hi[t - 1:t]], axis=1)
            hh = jnp.concatenate([hr, hi], axis=1).astype(BF16)
            y = jnp.dot(hh, cbd_ref[k], preferred_element_type=F32) + d_ref[:, cols] * uk
            y_ref[pl.ds(r0, t), cols] = y
        return 0

    lax.fori_loop(0, tr // t, chunk, 0)
    hl_ref[...] = hc_ref[...]


def _s5_scan(u, bbd, cbd, npow, ppow, d, h0, tr=256):
    l = u.shape[0]
    c3 = lambda i: (0, 0, 0)
    state = pl.BlockSpec((N_SLABS, 1, 2 * SLAB_STATE), c3)
    tabspec = pl.BlockSpec((N_SLABS, S5_CHUNK, 2 * SLAB_STATE), c3)
    return pl.pallas_call(
        functools.partial(_s5_scan_kernel, tr=tr),
        grid=(l // tr,),
        in_specs=[pl.BlockSpec((tr, D_MODEL), lambda i: (i, 0)),
                  pl.BlockSpec((N_SLABS, LANES, 2 * SLAB_STATE), c3),
                  pl.BlockSpec((N_SLABS, 2 * SLAB_STATE, LANES), c3),
                  tabspec, tabspec,
                  pl.BlockSpec((1, D_MODEL), lambda i: (0, 0)),
                  state],
        out_specs=[pl.BlockSpec((tr, D_MODEL), lambda i: (i, 0)), state],
        out_shape=[jax.ShapeDtypeStruct((l, D_MODEL), F32),
                   jax.ShapeDtypeStruct((N_SLABS, 1, 2 * SLAB_STATE), F32)],
        scratch_shapes=[pltpu.VMEM((N_SLABS, 1, 2 * SLAB_STATE), F32)],
        compiler_params=_cparams(("arbitrary",)),
        name="s5_scan_prompt",
    )(u, bbd, cbd, npow, ppow, d, h0)


def _s5_step_kernel(u_ref, bbd_ref, cbd_ref, pp_ref, d_ref, h0_ref, y_ref, hl_ref, *, nb, nt):
    half = SLAB_STATE
    for k in range(N_SLABS):
        cols = slice(k * LANES, (k + 1) * LANES)
        uk = u_ref[:, cols]
        bu = jnp.dot(uk.astype(BF16), bbd_ref[k], preferred_element_type=F32)
        ar, ai = pp_ref[k, 1:2, 0:half], pp_ref[k, 1:2, half:]
        h = h0_ref[k]
        hr, hi = h[:, 0:half], h[:, half:]
        hs = []
        for t in range(nt):
            rows = slice(t * nb, (t + 1) * nb)
            gr, gi = _cmul(ar, ai, hr, hi)
            hr, hi = gr + bu[rows, 0:half], gi + bu[rows, half:]
            hs.append(jnp.concatenate([hr, hi], axis=1))
        hl_ref[k] = hs[-1]
        hh = jnp.concatenate(hs, axis=0).astype(BF16)
        y_ref[:, cols] = jnp.dot(hh, cbd_ref[k], preferred_element_type=F32) + d_ref[:, cols] * uk


def _s5_steps(u_tm, bbd, cbd, ppow, d, h0, nb, nt):
    return pl.pallas_call(
        functools.partial(_s5_step_kernel, nb=nb, nt=nt),
        out_shape=[jax.ShapeDtypeStruct((nt * nb, D_MODEL), F32),
                   jax.ShapeDtypeStruct((N_SLABS, nb, 2 * SLAB_STATE), F32)],
        compiler_params=pltpu.CompilerParams(vmem_limit_bytes=VMEM_LIMIT),
        name="s5_scan_sample",
    )(u_tm, bbd, cbd, ppow, d, h0)


def _s5_out_kernel(y_ref, z_ref, x_ref, wg_ref, bg_ref, wo_ref, g_ref, b_ref, o_ref):
    g = jax.nn.gelu(y_ref[...])
    gate = jax.nn.sigmoid(jnp.dot(g.astype(BF16), wg_ref[...], preferred_element_type=F32) + bg_ref[...])
    a = (g * gate * _silu(z_ref[...])).astype(BF16)
    sub = jnp.dot(a, wo_ref[...], preferred_element_type=F32)
    o_ref[...] = _layer_norm(ALPHA * x_ref[...] + sub, g_ref[...], b_ref[...])


def _s5_out(y, z, x, w_glu, b_glu, w_out, g, b, tm):
    m = x.shape[0]
    wide = pl.BlockSpec((tm, D_MODEL), lambda i: (i, 0))
    const = lambda i: (0, 0)
    mat = pl.BlockSpec((D_MODEL, D_MODEL), const)
    vec = pl.BlockSpec((1, D_MODEL), const)
    return pl.pallas_call(
        _s5_out_kernel,
        grid=(m // tm,),
        in_specs=[wide, wide, wide, mat, vec, mat, vec, vec],
        out_specs=wide,
        out_shape=jax.ShapeDtypeStruct((m, D_MODEL), F32),
        compiler_params=_cparams(("arbitrary",)),
        name="s5_out_ln",
    )(y, z, x, w_glu, b_glu, w_out, g, b)


def _shift_pages(a, d):
    return jnp.concatenate([a[d:], jnp.zeros((d,) + a.shape[1:], a.dtype)], axis=0)


def _past_bias_kernel(pt_ref, lf_hbm, y_ref, buf_ref, sem, *, n_pages):
    b = pl.program_id(0)

    def page_copy(i):
        return pltpu.make_async_copy(lf_hbm.at[pt_ref[b * n_pages + i]], buf_ref.at[i], sem)

    def start(i, _):
        page_copy(i).start()
        return 0

    def wait(i, _):
        page_copy(i).wait()
        return 0

    lax.fori_loop(0, n_pages, start, 0)
    lax.fori_loop(0, n_pages, wait, 0)

    x = buf_ref[...]
    lane = lax.broadcasted_iota(jnp.int32, (1, 1, PAGE), 2)
    incl = x
    tot = x
    d = 1
    while d < PAGE:
        shifted = pltpu.roll(incl, PAGE - d, axis=2)
        incl = incl + jnp.where(lane < PAGE - d, shifted, 0.0)
        tot = tot + pltpu.roll(tot, d, axis=2)
        d *= 2
    excl = jnp.where(lane < PAGE - 1, pltpu.roll(incl, PAGE - 1, axis=2), 0.0)
    after = tot
    d = 1
    while d < n_pages:
        after = after + _shift_pages(after, d)
        d *= 2
    y_ref[0] = excl + _shift_pages(after, 1)


def _past_bias(page_table, cache_logf_l):
    nb, n_pages = page_table.shape
    lft = jnp.transpose(cache_logf_l, (0, 2, 1))
    return pl.pallas_call(
        functools.partial(_past_bias_kernel, n_pages=n_pages),
        grid_spec=pltpu.PrefetchScalarGridSpec(
            num_scalar_prefetch=1,
            grid=(nb,),
            in_specs=[pl.BlockSpec(memory_space=pl.ANY)],
            out_specs=pl.BlockSpec((1, n_pages, N_HEADS, PAGE), lambda b, pt: (b, 0, 0, 0)),
            scratch_shapes=[pltpu.VMEM((n_pages, N_HEADS, PAGE), F32), pltpu.SemaphoreType.DMA(())]),
        out_shape=jax.ShapeDtypeStruct((nb, n_pages, N_HEADS, PAGE), F32),
        compiler_params=_cparams(("arbitrary",)),
        name="past_bias",
    )(page_table.reshape(-1), lft)


def _decode_kernel(pt_ref, q_ref, kn_ref, vn_ref, lfn_ref, kc_ref, vc_ref, yb_ref, o_ref,
                   qbd_ref, m_ref, l_ref, acc_ref, *, nt, n_pages):
    j = pl.program_id(1)
    rows = nt * N_HEADS
    rsel = lax.broadcasted_iota(jnp.int32, (rows, N_HEADS), 0)
    csel = lax.broadcasted_iota(jnp.int32, (rows, N_HEADS), 1)
    sel = jnp.where((rsel & (N_HEADS - 1)) == csel, 1.0, 0.0).astype(BF16)

    @pl.when(j == 0)
    def _():
        hrow = lax.broadcasted_iota(jnp.int32, (N_HEADS, D_MODEL), 0)
        hcol = lax.broadcasted_iota(jnp.int32, (N_HEADS, D_MODEL), 1)
        own = (hcol // HEAD_DIM) == hrow
        q = q_ref[0].astype(F32)
        qbd = [jnp.where(own, jnp.broadcast_to(q[t:t + 1], (N_HEADS, D_MODEL)), 0.0) for t in range(nt)]
        qbd_ref[...] = jnp.concatenate(qbd, axis=0).astype(BF16)

        key_row = lax.broadcasted_iota(jnp.int32, (PAGE, D_MODEL), 0)
        key_row_h = lax.broadcasted_iota(jnp.int32, (PAGE, N_HEADS), 0)
        kn, vn, lfn = kn_ref[0], vn_ref[0], lfn_ref[0]
        kpad = jnp.zeros((PAGE, D_MODEL), F32)
        vpad = jnp.zeros((PAGE, D_MODEL), F32)
        ypad = jnp.zeros((PAGE, N_HEADS), F32)
        cn = jnp.zeros((1, N_HEADS), F32)
        for t in range(nt):
            cn = cn + lfn[t:t + 1]
            kpad = jnp.where(key_row == t, jnp.broadcast_to(kn[t:t + 1], (PAGE, D_MODEL)), kpad)
            vpad = jnp.where(key_row == t, jnp.broadcast_to(vn[t:t + 1], (PAGE, D_MODEL)), vpad)
            ypad = jnp.where(key_row_h == t, jnp.broadcast_to(-cn, (PAGE, N_HEADS)), ypad)
        s = lax.dot_general(qbd_ref[...], kpad.astype(BF16), NT_DIMS, preferred_element_type=F32)
        for part in _split3(ypad):
            s = s + lax.dot_general(sel, part, NT_DIMS, preferred_element_type=F32)
        srow = lax.broadcasted_iota(jnp.int32, (rows, PAGE), 0)
        scol = lax.broadcasted_iota(jnp.int32, (rows, PAGE), 1)
        s = jnp.where(scol <= (srow // N_HEADS), s, -jnp.inf)
        m = jnp.max(s, axis=1, keepdims=True)
        pr = jnp.exp(s - m)
        m_ref[...] = m
        l_ref[...] = jnp.sum(pr, axis=1, keepdims=True)
        acc_ref[...] = jnp.dot(pr.astype(BF16), vpad.astype(BF16), preferred_element_type=F32)

    @pl.when(j > 0)
    def _():
        s = jnp.dot(qbd_ref[...], kc_ref[0].astype(BF16), preferred_element_type=F32)
        yb = yb_ref[0, 0]
        s = s + jnp.concatenate([yb] * nt, axis=0)
        m = m_ref[...]
        m_new = jnp.maximum(m, jnp.max(s, axis=1, keepdims=True))
        corr = jnp.exp(m - m_new)
        pr = jnp.exp(s - m_new)
        m_ref[...] = m_new
        l_ref[...] = l_ref[...] * corr + jnp.sum(pr, axis=1, keepdims=True)
        pv = lax.dot_general(pr.astype(BF16), vc_ref[0].astype(BF16), NT_DIMS, preferred_element_type=F32)
        acc_ref[...] = acc_ref[...] * corr + pv

    @pl.when(j == n_pages)
    def _():
        hrow = lax.broadcasted_iota(jnp.int32, (N_HEADS, D_MODEL), 0)
        hcol = lax.broadcasted_iota(jnp.int32, (N_HEADS, D_MODEL), 1)
        own = (hcol // HEAD_DIM) == hrow
        o = acc_ref[...] / l_ref[...]
        for t in range(nt):
            blk = jnp.where(own, o[t * N_HEADS:(t + 1) * N_HEADS], 0.0)
            o_ref[0, t:t + 1, :] = jnp.sum(blk, axis=0, keepdims=True)


def _decode_attn(page_table, q, k_new, v_new, lf_new, cache_k_l, cache_v_l, ybias):
    nb, n_pages = page_table.shape
    nt = q.shape[1]
    rows = nt * N_HEADS
    n_pool = cache_k_l.shape[0]
    kc = jnp.transpose(cache_k_l, (0, 2, 3, 1)).reshape(n_pool, D_MODEL, PAGE)
    vc = jnp.transpose(cache_v_l, (0, 2, 3, 1)).reshape(n_pool, D_MODEL, PAGE)

    def logical(j):
        return n_pages - jnp.maximum(j, 1)

    per_b = lambda b, j, pt: (b, 0, 0)
    page = lambda b, j, pt: (pt[b * n_pages + logical(j)], 0, 0)
    return pl.pallas_call(
        functools.partial(_decode_kernel, nt=nt, n_pages=n_pages),
        grid_spec=pltpu.PrefetchScalarGridSpec(
            num_scalar_prefetch=1,
            grid=(nb, n_pages + 1),
            in_specs=[pl.BlockSpec((1, nt, D_MODEL), per_b),
                      pl.BlockSpec((1, nt, D_MODEL), per_b),
                      pl.BlockSpec((1, nt, D_MODEL), per_b),
                      pl.BlockSpec((1, nt, N_HEADS), per_b),
                      pl.BlockSpec((1, D_MODEL, PAGE), page),
                      pl.BlockSpec((1, D_MODEL, PAGE), page),
                      pl.BlockSpec((1, 1, N_HEADS, PAGE), lambda b, j, pt: (b, logical(j), 0, 0))],
            out_specs=pl.BlockSpec((1, nt, D_MODEL), per_b),
            scratch_shapes=[pltpu.VMEM((rows, D_MODEL), BF16),
                            pltpu.VMEM((rows, 1), F32),
                            pltpu.VMEM((rows, 1), F32),
                            pltpu.VMEM((rows, D_MODEL), F32)]),
        out_shape=jax.ShapeDtypeStruct((nb, nt, D_MODEL), F32),
        compiler_params=_cparams(("arbitrary", "arbitrary")),
        name="decode_attn",
    )(page_table.reshape(-1), q, k_new, v_new, lf_new, kc, vc, ybias)


def _row_tile(m):
    return 512 if m % 512 == 0 else m


def kernel(x_prompt, x_sample, cache_k, cache_v, cache_logf, page_table, state_s5_re, state_s5_im,
           attn_w_in, attn_b_f, attn_w_out, s5_w_in, s5_a_re, s5_a_im, s5_log_dt,
           s5_b_re, s5_b_im, s5_c_re, s5_c_im, s5_d, s5_w_glu, s5_b_glu, s5_w_out,
           ln_g, ln_b):
    assert x_prompt.shape[0] == 1
    l = x_prompt.shape[1]
    nb, nt = x_sample.shape[:2]
    width = N_HEADS * HEAD_DIM

    w_in = attn_w_in[0]
    w_qkvz = w_in[:, :4 * width].astype(BF16)
    w_qvz = jnp.concatenate([w_in[:, 0:width], w_in[:, 2 * width:4 * width]], axis=1).astype(BF16)
    wt_kv = w_in[:, width:3 * width].T.astype(BF16)
    w_f = jnp.pad(w_in[:, 4 * width:], ((0, 0), (0, LANES - N_HEADS))).astype(BF16)
    b_f = jnp.pad(attn_b_f[0], (0, LANES - N_HEADS)).reshape(1, LANES)
    w_ao = attn_w_out[0].astype(BF16)
    w_s5in = s5_w_in[0].astype(BF16)
    w_glu = s5_w_glu[0].astype(BF16)
    b_glu = s5_b_glu[0].reshape(1, D_MODEL)
    w_so = s5_w_out[0].astype(BF16)
    g0, b0 = ln_g[0].reshape(1, D_MODEL), ln_b[0].reshape(1, D_MODEL)
    g1, b1 = ln_g[1].reshape(1, D_MODEL), ln_b[1].reshape(1, D_MODEL)
    d_skip = s5_d[0].reshape(1, D_MODEL)

    pre, pim, nre, nim, bb_re, bb_im = _s5_params(s5_a_re[0], s5_a_im[0], s5_log_dt[0],
                                                  s5_b_re[0], s5_b_im[0])
    ppow = _slab_table(pre, pim)
    npow = _slab_table(nre, nim)
    bbd, cbd = _s5_weights(bb_re, bb_im, s5_c_re[0], s5_c_im[0])

    def to_slab_state(re, im):
        f = lambda a: a.reshape(a.shape[0], N_SLABS, SLAB_STATE).transpose(1, 0, 2)
        return jnp.concatenate([f(re), f(im)], axis=-1)

    def from_slab_state(h):
        f = lambda a: a.transpose(1, 0, 2).reshape(a.shape[1], N_SLABS * GROUPS_PER_SLAB, S5_STATE)
        return f(h[..., :SLAB_STATE]), f(h[..., SLAB_STATE:])

    xp = x_prompt[0]
    tm = _row_tile(l)
    q, vb, z, kt, ktb, vt, lf = _attn_in_proj_prompt(xp, w_qvz, wt_kv, w_f, b_f, tm)
    ct = _cumsum_t(lf)
    o = _attn_prompt(q, ktb, vb, ct)
    x1 = _attn_out(o, z, xp, w_ao, g0, b0, tm)
    u, z1 = _s5_in_proj(x1, w_s5in, tm)
    zero_state = jnp.zeros((N_SLABS, 1, 2 * SLAB_STATE), F32)
    y, h_last = _s5_scan(u, bbd, cbd, npow, ppow, d_skip, zero_state)
    y_prompt = _s5_out(y, z1, x1, w_glu, b_glu, w_so, g1, b1, tm)
    hp_re, hp_im = from_slab_state(h_last)

    ms = nb * nt
    xs = x_sample.reshape(ms, D_MODEL)
    qs, ks, vs, _, _, zs, lfs = _attn_in_proj(xs, w_qkvz, w_f, b_f, ms)
    ybias = _past_bias(page_table, cache_logf[0])
    os_ = _decode_attn(page_table, qs.reshape(nb, nt, D_MODEL), ks.reshape(nb, nt, D_MODEL),
                       vs.reshape(nb, nt, D_MODEL), lfs.reshape(nb, nt, N_HEADS),
                       cache_k[0], cache_v[0], ybias)
    xs1 = _attn_out(os_.reshape(ms, D_MODEL), zs, xs, w_ao, g0, b0, ms)
    us, zs1 = _s5_in_proj(xs1, w_s5in, ms)
    tmaj = lambda a: a.reshape(nb, nt, D_MODEL).transpose(1, 0, 2).reshape(ms, D_MODEL)
    bmaj = lambda a: a.reshape(nt, nb, D_MODEL).transpose(1, 0, 2).reshape(ms, D_MODEL)
    h0s = to_slab_state(state_s5_re[0], state_s5_im[0])
    ys_tm, hs_last = _s5_steps(tmaj(us), bbd, cbd, ppow, d_skip, h0s, nb, nt)
    y_sample = _s5_out(bmaj(ys_tm), zs1, xs1, w_glu, b_glu, w_so, g1, b1, ms)
    hs_re, hs_im = from_slab_state(hs_last)

    kv5 = lambda a, bsz, t: a.reshape(1, bsz, t, N_HEADS, HEAD_DIM)
    from_t = lambda a: a.reshape(N_HEADS, HEAD_DIM, l).transpose(2, 0, 1)[None, None]
    return (y_prompt.reshape(1, l, D_MODEL),
            y_sample.reshape(nb, nt, D_MODEL),
            from_t(kt), from_t(vt), lf.reshape(1, 1, l, N_HEADS),
            kv5(ks, nb, nt), kv5(vs, nb, nt), lfs.reshape(1, nb, nt, N_HEADS),
            hp_re[None], hp_im[None], hs_re[None], hs_im[None])
```

```python
import functools
import math

import jax
import jax.numpy as jnp
from jax import lax
from jax.experimental import pallas as pl
from jax.experimental.pallas import tpu as pltpu

F32 = jnp.float32
BF16 = jnp.bfloat16

D_MODEL = 1024
N_HEADS = 16
HEAD_DIM = 64
PAGE = 128
DEPTH = 2
ALPHA = (2 * DEPTH) ** 0.25
LN_EPS = 1e-5
QK_SCALE = HEAD_DIM ** -0.5
LOG2E = math.log2(math.e)
DEAD_GAP = 152.0
BOUND_SLACK = 2.0

LANES = 128
N_SLABS = D_MODEL // LANES
S5_GROUP = 16
S5_STATE = 64
GROUPS_PER_SLAB = LANES // S5_GROUP
SLAB_STATE = GROUPS_PER_SLAB * S5_STATE
S5_CHUNK = 64
DECODE_GROUP = 8
VMEM_LIMIT = 56 * 1024 * 1024

NT_DIMS = (((1,), (1,)), ((), ()))


def _cparams(sem):
    return pltpu.CompilerParams(dimension_semantics=sem, vmem_limit_bytes=VMEM_LIMIT)


def _split2(x):
    hi = x.astype(BF16)
    lo = (x - hi.astype(F32)).astype(BF16)
    return hi, lo


def _split3(x):
    hi = x.astype(BF16)
    r = x - hi.astype(F32)
    mid = r.astype(BF16)
    lo = (r - mid.astype(F32)).astype(BF16)
    return hi, mid, lo


def _layer_norm(r, g, b):
    mu = jnp.mean(r, axis=-1, keepdims=True)
    c = r - mu
    var = jnp.mean(c * c, axis=-1, keepdims=True)
    return c * lax.rsqrt(var + LN_EPS) * g + b


def _silu(z):
    return z * jax.nn.sigmoid(z)


def _attn_in_kernel(x_ref, w_ref, wf_ref, bf_ref, q_ref, k_ref, v_ref, kb_ref, vb_ref, z_ref, lf_ref):
    xb = x_ref[...].astype(BF16)
    q = jnp.dot(xb, w_ref[:, 0:D_MODEL], preferred_element_type=F32)
    q_ref[...] = (q * QK_SCALE).astype(BF16)
    k = jnp.dot(xb, w_ref[:, D_MODEL:2 * D_MODEL], preferred_element_type=F32)
    k_ref[...] = k
    kb_ref[...] = k.astype(BF16)
    v = jnp.dot(xb, w_ref[:, 2 * D_MODEL:3 * D_MODEL], preferred_element_type=F32)
    v_ref[...] = v
    vb_ref[...] = v.astype(BF16)
    z_ref[...] = jnp.dot(xb, w_ref[:, 3 * D_MODEL:4 * D_MODEL], preferred_element_type=F32)
    hf = jnp.dot(xb, wf_ref[...], preferred_element_type=F32) + bf_ref[...]
    lf = jnp.minimum(hf, 0.0) - jnp.log(1.0 + jnp.exp(-jnp.abs(hf)))
    lf_ref[...] = lf[:, 0:N_HEADS]


def _attn_in_proj(x, w_qkvz, w_f, b_f, tm):
    m = x.shape[0]
    row = lambda i: (i, 0)
    const = lambda i: (0, 0)
    wide = pl.BlockSpec((tm, D_MODEL), row)
    return pl.pallas_call(
        _attn_in_kernel,
        grid=(m // tm,),
        in_specs=[wide,
                  pl.BlockSpec((D_MODEL, 4 * D_MODEL), const),
                  pl.BlockSpec((D_MODEL, LANES), const),
                  pl.BlockSpec((1, LANES), const)],
        out_specs=[wide, wide, wide, wide, wide, wide, pl.BlockSpec((tm, N_HEADS), row)],
        out_shape=[jax.ShapeDtypeStruct((m, D_MODEL), BF16),
                   jax.ShapeDtypeStruct((m, D_MODEL), F32),
                   jax.ShapeDtypeStruct((m, D_MODEL), F32),
                   jax.ShapeDtypeStruct((m, D_MODEL), BF16),
                   jax.ShapeDtypeStruct((m, D_MODEL), BF16),
                   jax.ShapeDtypeStruct((m, D_MODEL), F32),
                   jax.ShapeDtypeStruct((m, N_HEADS), F32)],
        compiler_params=_cparams(("arbitrary",)),
        name="attn_in_proj",
    )(x, w_qkvz, w_f, b_f)


def _attn_in_prompt_kernel(x_ref, w_ref, wt_ref, wf_ref, bf_ref,
                           q_ref, vb_ref, z_ref, kt_ref, ktb_ref, vt_ref, lf_ref, kn_ref):
    xb = x_ref[...].astype(BF16)
    q = jnp.dot(xb, w_ref[:, 0:D_MODEL], preferred_element_type=F32)
    q_ref[...] = (q * (QK_SCALE * LOG2E)).astype(BF16)
    vb_ref[...] = jnp.dot(xb, w_ref[:, D_MODEL:2 * D_MODEL], preferred_element_type=F32).astype(BF16)
    z_ref[...] = jnp.dot(xb, w_ref[:, 2 * D_MODEL:3 * D_MODEL], preferred_element_type=F32)
    kt = lax.dot_general(wt_ref[0:D_MODEL, :], xb, NT_DIMS, preferred_element_type=F32)
    kt_ref[...] = kt
    ktb = kt.astype(BF16)
    ktb_ref[...] = ktb
    kf = ktb.astype(F32)
    norm2 = jnp.sum((kf * kf).reshape(N_HEADS, HEAD_DIM, kf.shape[1]), axis=1)
    blockmax = jnp.max(norm2, axis=1, keepdims=True)

    @pl.when(pl.program_id(0) == 0)
    def _():
        kn_ref[...] = jnp.zeros_like(kn_ref)

    kn_ref[...] = jnp.maximum(kn_ref[...], blockmax)
    vt_ref[...] = lax.dot_general(wt_ref[D_MODEL:2 * D_MODEL, :], xb, NT_DIMS, preferred_element_type=F32)
    hf = jnp.dot(xb, wf_ref[...], preferred_element_type=F32) + bf_ref[...]
    lf = jnp.minimum(hf, 0.0) - jnp.log(1.0 + jnp.exp(-jnp.abs(hf)))
    lf_ref[...] = lf[:, 0:N_HEADS]


def _attn_in_proj_prompt(x, w_qvz, wt_kv, w_f, b_f, tm):
    m = x.shape[0]
    row = lambda i: (i, 0)
    col = lambda i: (0, i)
    const = lambda i: (0, 0)
    wide = pl.BlockSpec((tm, D_MODEL), row)
    tall = pl.BlockSpec((D_MODEL, tm), col)
    return pl.pallas_call(
        _attn_in_prompt_kernel,
        grid=(m // tm,),
        in_specs=[wide,
                  pl.BlockSpec((D_MODEL, 3 * D_MODEL), const),
                  pl.BlockSpec((2 * D_MODEL, D_MODEL), const),
                  pl.BlockSpec((D_MODEL, LANES), const),
                  pl.BlockSpec((1, LANES), const)],
        out_specs=[wide, wide, wide, tall, tall, tall, pl.BlockSpec((tm, N_HEADS), row),
                   pl.BlockSpec((N_HEADS, LANES), const)],
        out_shape=[jax.ShapeDtypeStruct((m, D_MODEL), BF16),
                   jax.ShapeDtypeStruct((m, D_MODEL), BF16),
                   jax.ShapeDtypeStruct((m, D_MODEL), F32),
                   jax.ShapeDtypeStruct((D_MODEL, m), F32),
                   jax.ShapeDtypeStruct((D_MODEL, m), BF16),
                   jax.ShapeDtypeStruct((D_MODEL, m), F32),
                   jax.ShapeDtypeStruct((m, N_HEADS), F32),
                   jax.ShapeDtypeStruct((N_HEADS, LANES), F32)],
        compiler_params=_cparams(("arbitrary",)),
        name="attn_in_proj_prompt",
    )(x, w_qvz, wt_kv, w_f, b_f)


def _cumsum_kernel(lf_ref, ct_ref, carry_ref, *, tc):
    @pl.when(pl.program_id(0) == 0)
    def _():
        carry_ref[...] = jnp.zeros_like(carry_ref)

    r = lax.broadcasted_iota(jnp.int32, (tc, tc), 0)
    c = lax.broadcasted_iota(jnp.int32, (tc, tc), 1)
    tri = jnp.where(r >= c, 1.0, 0.0).astype(BF16)
    acc = carry_ref[...]
    for part in _split3(lf_ref[...]):
        acc = acc + jnp.dot(tri, part, preferred_element_type=F32)
    carry_ref[...] = acc[tc - 1:tc, :]
    hr = lax.broadcasted_iota(jnp.int32, (N_HEADS, N_HEADS), 0)
    hc = lax.broadcasted_iota(jnp.int32, (N_HEADS, N_HEADS), 1)
    eye = jnp.where(hr == hc, 1.0, 0.0).astype(BF16)
    out = jnp.zeros((N_HEADS, tc), F32)
    for part in _split3(acc):
        out = out + lax.dot_general(eye, part, NT_DIMS, preferred_element_type=F32)
    ct_ref[...] = out


def _cumsum_t(logf, tc=256):
    l = logf.shape[0]
    return pl.pallas_call(
        functools.partial(_cumsum_kernel, tc=tc),
        grid=(l // tc,),
        in_specs=[pl.BlockSpec((tc, N_HEADS), lambda i: (i, 0))],
        out_specs=pl.BlockSpec((N_HEADS, tc), lambda i: (0, i)),
        out_shape=jax.ShapeDtypeStruct((N_HEADS, l), F32),
        scratch_shapes=[pltpu.VMEM((1, N_HEADS), F32)],
        compiler_params=_cparams(("arbitrary",)),
        name="logf_cumsum",
    )(logf)


def _attn_prompt_kernel(q_ref, kt_ref, v_ref, ct_ref, kn_ref, o_ref, *, tq):
    p = pl.program_id(0)
    qi = pl.program_id(1)
    q = q_ref[...]
    lane = lax.broadcasted_iota(jnp.int32, (1, LANES), 1)
    first = lane < HEAD_DIM
    zero = jnp.zeros_like(q)
    qs = (jnp.where(first, q, zero), jnp.where(first, zero, q))
    ones_col = (jnp.where(lane == HEAD_DIM, 1.0, 0.0).astype(BF16), jnp.where(lane == 0, 1.0, 0.0).astype(BF16))
    q0 = pl.multiple_of(qi * tq, tq)
    c0 = [ct_ref[pl.ds(2 * p + e, 1), pl.ds(q0, tq)][:, 0:1] for e in range(2)]
    row = lax.broadcasted_iota(jnp.int32, (tq, tq), 0)
    col = lax.broadcasted_iota(jnp.int32, (tq, tq), 1)
    causal = col <= row

    def block(j, carry, masked):
        k0 = pl.multiple_of(j * tq, tq)
        kt = kt_ref[:, pl.ds(k0, tq)]
        v = v_ref[pl.ds(k0, tq), :]
        vs = (jnp.where(first, v, ones_col[0]), jnp.where(first, ones_col[1], v))
        out = []
        for e in range(2):
            m, acc = carry[e]
            bias = (c0[e] - ct_ref[pl.ds(2 * p + e, 1), pl.ds(k0, tq)]) * LOG2E
            s = jnp.dot(qs[e], kt, preferred_element_type=F32) + bias
            if masked:
                s = jnp.where(causal, s, -jnp.inf)
            m_new = jnp.maximum(m, jnp.max(s, axis=1, keepdims=True))
            pr = jnp.exp2(s - m_new).astype(BF16)
            acc = jnp.exp2(m - m_new) * acc + jnp.dot(pr, vs[e], preferred_element_type=F32)
            out.append((m_new, acc))
        return tuple(out)

    init = tuple((jnp.full((tq, 1), -jnp.inf, F32), jnp.zeros((tq, LANES), F32)) for _ in range(2))
    carry = block(qi, init, True)

    pos = lax.broadcasted_iota(jnp.int32, (1, ct_ref.shape[1]), 1)
    n_blocks = 0
    for e in range(2):
        m_min = jnp.min(carry[e][0], axis=0, keepdims=True)
        qf = qs[e].astype(F32)
        qn2 = jnp.max(jnp.sum(qf * qf, axis=1, keepdims=True), axis=0, keepdims=True)
        kn2 = kn_ref[pl.ds(2 * p + e, 1), :][:, 0:1]
        reach = jnp.sqrt(qn2 * kn2) + BOUND_SLACK
        upper = (c0[e] - ct_ref[pl.ds(2 * p + e, 1), :]) * LOG2E + reach
        live = jnp.logical_and(upper >= m_min - DEAD_GAP, pos < q0)
        count = jnp.sum(jnp.where(live, 1.0, 0.0)).astype(jnp.int32)
        n_blocks = jnp.maximum(n_blocks, (count + (tq - 1)) // tq)
    carry = lax.fori_loop(0, n_blocks, lambda t, c: block(qi - 1 - t, c, False), carry)
    (_, acca), (_, accb) = carry
    o_ref[...] = jnp.where(first, acca / acca[:, HEAD_DIM:HEAD_DIM + 1], accb / accb[:, 0:1])


def _attn_prompt(q, ktb, vb, ct, kn2, tq=512):
    l = q.shape[0]
    return pl.pallas_call(
        functools.partial(_attn_prompt_kernel, tq=tq),
        grid=(N_SLABS, l // tq),
        in_specs=[pl.BlockSpec((tq, LANES), lambda p, i: (i, p)),
                  pl.BlockSpec((LANES, l), lambda p, i: (p, 0)),
                  pl.BlockSpec((l, LANES), lambda p, i: (0, p)),
                  pl.BlockSpec((N_HEADS, l), lambda p, i: (0, 0)),
                  pl.BlockSpec((N_HEADS, LANES), lambda p, i: (0, 0))],
        out_specs=pl.BlockSpec((tq, LANES), lambda p, i: (i, p)),
        out_shape=jax.ShapeDtypeStruct((l, D_MODEL), F32),
        compiler_params=_cparams(("arbitrary", "arbitrary")),
        name="attn_prompt",
    )(q, ktb, vb, ct, kn2)


def _attn_out_kernel(o_ref, z_ref, x_ref, w_ref, g_ref, b_ref, y_ref):
    a = (o_ref[...] * _silu(z_ref[...])).astype(BF16)
    sub = jnp.dot(a, w_ref[...], preferred_element_type=F32)
    y_ref[...] = _layer_norm(ALPHA * x_ref[...] + sub, g_ref[...], b_ref[...])


def _attn_out(o, z, x, w_out, g, b, tm):
    m = x.shape[0]
    wide = pl.BlockSpec((tm, D_MODEL), lambda i: (i, 0))
    const = lambda i: (0, 0)
    return pl.pallas_call(
        _attn_out_kernel,
        grid=(m // tm,),
        in_specs=[wide, wide, wide, pl.BlockSpec((D_MODEL, D_MODEL), const),
                  pl.BlockSpec((1, D_MODEL), const), pl.BlockSpec((1, D_MODEL), const)],
        out_specs=wide,
        out_shape=jax.ShapeDtypeStruct((m, D_MODEL), F32),
        compiler_params=_cparams(("arbitrary",)),
        name="attn_out_ln",
    )(o, z, x, w_out, g, b)


def _s5_in_kernel(x_ref, w_ref, u_ref, z_ref):
    xb = x_ref[...].astype(BF16)
    u_ref[...] = jnp.dot(xb, w_ref[:, 0:D_MODEL], preferred_element_type=F32)
    z_ref[...] = jnp.dot(xb, w_ref[:, D_MODEL:2 * D_MODEL], preferred_element_type=F32)


def _s5_in_proj(x, w_in, tm):
    m = x.shape[0]
    wide = pl.BlockSpec((tm, D_MODEL), lambda i: (i, 0))
    return pl.pallas_call(
        _s5_in_kernel,
        grid=(m // tm,),
        in_specs=[wide, pl.BlockSpec((D_MODEL, 2 * D_MODEL), lambda i: (0, 0))],
        out_specs=[wide, wide],
        out_shape=[jax.ShapeDtypeStruct((m, D_MODEL), F32)] * 2,
        compiler_params=_cparams(("arbitrary",)),
        name="s5_in_proj",
    )(x, w_in)


def _s5_param_kernel(are_ref, aim_ref, ldt_ref, arer_ref, aimr_ref, ldtr_ref, bre_ref, bim_ref,
                     pre_ref, pim_ref, nre_ref, nim_ref, bbre_ref, bbim_ref):
    def abar(are, aim, ldt):
        dt = jnp.exp(ldt)
        mag = jnp.exp(are * dt)
        ang = aim * dt
        return mag * jnp.cos(ang), mag * jnp.sin(ang)

    are, aim = are_ref[...], aim_ref[...]
    ar, ai = abar(are, aim, ldt_ref[...])
    inv = 1.0 / (ar * ar + ai * ai)
    nr, ni = ar * inv, -ai * inv
    pr, pi = jnp.ones_like(ar), jnp.zeros_like(ar)
    qr, qi = pr, pi
    for t in range(S5_CHUNK):
        pre_ref[t] = pr
        pim_ref[t] = pi
        nre_ref[t] = qr
        nim_ref[t] = qi
        pr, pi = pr * ar - pi * ai, pr * ai + pi * ar
        qr, qi = qr * nr - qi * ni, qr * ni + qi * nr

    lr, li = arer_ref[...], aimr_ref[...]
    er, ei = abar(lr, li, ldtr_ref[...])
    linv = 1.0 / (lr * lr + li * li)
    fr = ((er - 1.0) * lr + ei * li) * linv
    fi = (ei * lr - (er - 1.0) * li) * linv
    br, bi = bre_ref[...], bim_ref[...]
    bbre_ref[...] = fr * br - fi * bi
    bbim_ref[...] = fr * bi + fi * br


def _s5_params(a_re, a_im, log_dt, b_re, b_im):
    g, p = a_re.shape
    rep = lambda a: jnp.repeat(a, S5_GROUP, axis=0)
    ldt = jnp.broadcast_to(log_dt[:, None], (g, p))
    b_re_t = b_re.transpose(0, 2, 1).reshape(g * S5_GROUP, p)
    b_im_t = b_im.transpose(0, 2, 1).reshape(g * S5_GROUP, p)
    tab = jax.ShapeDtypeStruct((S5_CHUNK, g, p), F32)
    big = jax.ShapeDtypeStruct((g * S5_GROUP, p), F32)
    return pl.pallas_call(
        _s5_param_kernel,
        out_shape=[tab, tab, tab, tab, big, big],
        name="s5_discretise",
    )(a_re, a_im, ldt, rep(a_re), rep(a_im), rep(ldt), b_re_t, b_im_t)


def _slab_table(t_re, t_im):
    t = t_re.shape[0]
    f = lambda a: a.reshape(t, N_SLABS, SLAB_STATE).transpose(1, 0, 2)
    return jnp.concatenate([f(t_re), f(t_im)], axis=-1)


def _s5_weights(bb_re, bb_im, c_re, c_im):
    eye = jnp.eye(GROUPS_PER_SLAB, dtype=F32)
    shape_b = (N_SLABS, GROUPS_PER_SLAB, S5_GROUP, S5_STATE)

    def in_mat(b):
        return jnp.einsum('kgjp,gh->kgjhp', b.reshape(shape_b), eye).reshape(N_SLABS, LANES, SLAB_STATE)

    def out_mat(c):
        return jnp.einsum('kgip,gh->kgphi', c.reshape(shape_b), eye).reshape(N_SLABS, SLAB_STATE, LANES)

    bbd = jnp.concatenate([in_mat(bb_re), in_mat(bb_im)], axis=-1).astype(BF16)
    cbd = jnp.concatenate([out_mat(c_re), -out_mat(c_im)], axis=1).astype(BF16)
    return bbd, cbd


def _cmul(ar, ai, br, bi):
    return ar * br - ai * bi, ar * bi + ai * br


def _s5_scan_kernel(u_ref, bbd_ref, cbd_ref, np_ref, pp_ref, d_ref, h0_ref, y_ref, hl_ref, hc_ref,
                    *, tr):
    t = S5_CHUNK
    half = SLAB_STATE

    @pl.when(pl.program_id(0) == 0)
    def _():
        hc_ref[...] = h0_ref[...]

    r = lax.broadcasted_iota(jnp.int32, (t, t), 0)
    c = lax.broadcasted_iota(jnp.int32, (t, t), 1)
    tri = jnp.where(r >= c, 1.0, 0.0).astype(BF16)

    for k in range(N_SLABS):
        cols = slice(k * LANES, (k + 1) * LANES)
        uk = u_ref[:, cols]
        bu = jnp.dot(uk.astype(BF16), bbd_ref[k], preferred_element_type=F32)
        nr, ni = np_ref[k, :, 0:half], np_ref[k, :, half:]
        pr, pi = pp_ref[k, :, 0:half], pp_ref[k, :, half:]
        ar, ai = pp_ref[k, 1:2, 0:half], pp_ref[k, 1:2, half:]
        hp = hc_ref[k]
        hpr, hpi = hp[:, 0:half], hp[:, half:]
        hh = []
        for ci in range(tr // t):
            rows = slice(ci * t, (ci + 1) * t)
            zr, zi = _cmul(nr, ni, bu[rows, 0:half], bu[rows, half:])
            zz = jnp.concatenate([zr, zi], axis=1).astype(BF16)
            s = jnp.dot(tri, zz, preferred_element_type=F32)
            gr, gi = _cmul(ar, ai, hpr, hpi)
            hr, hi = _cmul(pr, pi, s[:, 0:half] + gr, s[:, half:] + gi)
            hpr, hpi = hr[t - 1:t], hi[t - 1:t]
            hh.append(jnp.concatenate([hr, hi], axis=1).astype(BF16))
        hc_ref[k] = jnp.concatenate([hpr, hpi], axis=1)
        y = jnp.dot(jnp.concatenate(hh, axis=0), cbd_ref[k], preferred_element_type=F32)
        y_ref[:, cols] = y + d_ref[:, cols] * uk
    hl_ref[...] = hc_ref[...]


def _s5_scan(u, bbd, cbd, npow, ppow, d, h0, tr=256):
    l = u.shape[0]
    c3 = lambda i: (0, 0, 0)
    state = pl.BlockSpec((N_SLABS, 1, 2 * SLAB_STATE), c3)
    tabspec = pl.BlockSpec((N_SLABS, S5_CHUNK, 2 * SLAB_STATE), c3)
    return pl.pallas_call(
        functools.partial(_s5_scan_kernel, tr=tr),
        grid=(l // tr,),
        in_specs=[pl.BlockSpec((tr, D_MODEL), lambda i: (i, 0)),
                  pl.BlockSpec((N_SLABS, LANES, 2 * SLAB_STATE), c3),
                  pl.BlockSpec((N_SLABS, 2 * SLAB_STATE, LANES), c3),
                  tabspec, tabspec,
                  pl.BlockSpec((1, D_MODEL), lambda i: (0, 0)),
                  state],
        out_specs=[pl.BlockSpec((tr, D_MODEL), lambda i: (i, 0)), state],
        out_shape=[jax.ShapeDtypeStruct((l, D_MODEL), F32),
                   jax.ShapeDtypeStruct((N_SLABS, 1, 2 * SLAB_STATE), F32)],
        scratch_shapes=[pltpu.VMEM((N_SLABS, 1, 2 * SLAB_STATE), F32)],
        compiler_params=_cparams(("arbitrary",)),
        name="s5_scan_prompt",
    )(u, bbd, cbd, npow, ppow, d, h0)


def _s5_step_kernel(u_ref, bbd_ref, cbd_ref, pp_ref, d_ref, h0_ref, y_ref, hl_ref, *, nb, nt):
    half = SLAB_STATE
    for k in range(N_SLABS):
        cols = slice(k * LANES, (k + 1) * LANES)
        uk = u_ref[:, cols]
        bu = jnp.dot(uk.astype(BF16), bbd_ref[k], preferred_element_type=F32)
        ar, ai = pp_ref[k, 1:2, 0:half], pp_ref[k, 1:2, half:]
        h = h0_ref[k]
        hr, hi = h[:, 0:half], h[:, half:]
        hs = []
        for t in range(nt):
            rows = slice(t * nb, (t + 1) * nb)
            gr, gi = _cmul(ar, ai, hr, hi)
            hr, hi = gr + bu[rows, 0:half], gi + bu[rows, half:]
            hs.append(jnp.concatenate([hr, hi], axis=1))
        hl_ref[k] = hs[-1]
        hh = jnp.concatenate(hs, axis=0).astype(BF16)
        y_ref[:, cols] = jnp.dot(hh, cbd_ref[k], preferred_element_type=F32) + d_ref[:, cols] * uk


def _s5_steps(u_tm, bbd, cbd, ppow, d, h0, nb, nt):
    return pl.pallas_call(
        functools.partial(_s5_step_kernel, nb=nb, nt=nt),
        out_shape=[jax.ShapeDtypeStruct((nt * nb, D_MODEL), F32),
                   jax.ShapeDtypeStruct((N_SLABS, nb, 2 * SLAB_STATE), F32)],
        compiler_params=pltpu.CompilerParams(vmem_limit_bytes=VMEM_LIMIT),
        name="s5_scan_sample",
    )(u_tm, bbd, cbd, ppow, d, h0)


def _s5_out_kernel(y_ref, z_ref, x_ref, wg_ref, bg_ref, wo_ref, g_ref, b_ref, o_ref):
    g = jax.nn.gelu(y_ref[...])
    gate = jax.nn.sigmoid(jnp.dot(g.astype(BF16), wg_ref[...], preferred_element_type=F32) + bg_ref[...])
    a = (g * gate * _silu(z_ref[...])).astype(BF16)
    sub = jnp.dot(a, wo_ref[...], preferred_element_type=F32)
    o_ref[...] = _layer_norm(ALPHA * x_ref[...] + sub, g_ref[...], b_ref[...])


def _s5_out(y, z, x, w_glu, b_glu, w_out, g, b, tm):
    m = x.shape[0]
    wide = pl.BlockSpec((tm, D_MODEL), lambda i: (i, 0))
    const = lambda i: (0, 0)
    mat = pl.BlockSpec((D_MODEL, D_MODEL), const)
    vec = pl.BlockSpec((1, D_MODEL), const)
    return pl.pallas_call(
        _s5_out_kernel,
        grid=(m // tm,),
        in_specs=[wide, wide, wide, mat, vec, mat, vec, vec],
        out_specs=wide,
        out_shape=jax.ShapeDtypeStruct((m, D_MODEL), F32),
        compiler_params=_cparams(("arbitrary",)),
        name="s5_out_ln",
    )(y, z, x, w_glu, b_glu, w_out, g, b)


def _shift_pages(a, d):
    return jnp.concatenate([a[d:], jnp.zeros((d,) + a.shape[1:], a.dtype)], axis=0)


def _past_bias_kernel(pt_ref, lf_hbm, y_ref, buf_ref, sem, *, n_pages):
    b = pl.program_id(0)

    def page_copy(i):
        return pltpu.make_async_copy(lf_hbm.at[pt_ref[b * n_pages + i]], buf_ref.at[i], sem)

    def start(i, _):
        page_copy(i).start()
        return 0

    def wait(i, _):
        page_copy(i).wait()
        return 0

    lax.fori_loop(0, n_pages, start, 0)
    lax.fori_loop(0, n_pages, wait, 0)

    x = buf_ref[...]
    lane = lax.broadcasted_iota(jnp.int32, (1, 1, PAGE), 2)
    incl = x
    tot = x
    d = 1
    while d < PAGE:
        shifted = pltpu.roll(incl, PAGE - d, axis=2)
        incl = incl + jnp.where(lane < PAGE - d, shifted, 0.0)
        tot = tot + pltpu.roll(tot, d, axis=2)
        d *= 2
    excl = jnp.where(lane < PAGE - 1, pltpu.roll(incl, PAGE - 1, axis=2), 0.0)
    after = tot
    d = 1
    while d < n_pages:
        after = after + _shift_pages(after, d)
        d *= 2
    y_ref[0] = excl + _shift_pages(after, 1)


def _past_bias(page_table, cache_logf_l):
    nb, n_pages = page_table.shape
    lft = jnp.transpose(cache_logf_l, (0, 2, 1))
    return pl.pallas_call(
        functools.partial(_past_bias_kernel, n_pages=n_pages),
        grid_spec=pltpu.PrefetchScalarGridSpec(
            num_scalar_prefetch=1,
            grid=(nb,),
            in_specs=[pl.BlockSpec(memory_space=pl.ANY)],
            out_specs=pl.BlockSpec((1, n_pages, N_HEADS, PAGE), lambda b, pt: (b, 0, 0, 0)),
            scratch_shapes=[pltpu.VMEM((n_pages, N_HEADS, PAGE), F32), pltpu.SemaphoreType.DMA(())]),
        out_shape=jax.ShapeDtypeStruct((nb, n_pages, N_HEADS, PAGE), F32),
        compiler_params=_cparams(("arbitrary",)),
        name="past_bias",
    )(page_table.reshape(-1), lft)


def _decode_kernel(pt_ref, q_ref, kn_ref, vn_ref, lfn_ref, kc_hbm, vc_hbm, yb_ref, o_ref,
                   kbuf, vbuf, sem, qbd_ref, m_ref, l_ref, acc_ref, *, nt, n_pages, group):
    b = pl.program_id(0)
    j = pl.program_id(1)
    n_steps = n_pages // group
    step = b * n_steps + j
    slot = step % 2
    rows = nt * N_HEADS
    rsel = lax.broadcasted_iota(jnp.int32, (rows, N_HEADS), 0)
    csel = lax.broadcasted_iota(jnp.int32, (rows, N_HEADS), 1)
    sel = jnp.where((rsel & (N_HEADS - 1)) == csel, 1.0, 0.0).astype(BF16)

    def page_copies(bb, jj, sl):
        out = []
        for g in range(group):
            page = pt_ref[bb * n_pages + n_pages - (jj + 1) * group + g]
            out.append(pltpu.make_async_copy(kc_hbm.at[page], kbuf.at[sl, g], sem.at[sl]))
            out.append(pltpu.make_async_copy(vc_hbm.at[page], vbuf.at[sl, g], sem.at[sl]))
        return out

    @pl.when(step == 0)
    def _():
        for cp in page_copies(0, 0, 0):
            cp.start()

    @pl.when(step + 1 < pl.num_programs(0) * n_steps)
    def _():
        wrap = j + 1 == n_steps
        for cp in page_copies(jnp.where(wrap, b + 1, b), jnp.where(wrap, 0, j + 1), 1 - slot):
            cp.start()

    @pl.when(j == 0)
    def _():
        hrow = lax.broadcasted_iota(jnp.int32, (N_HEADS, D_MODEL), 0)
        hcol = lax.broadcasted_iota(jnp.int32, (N_HEADS, D_MODEL), 1)
        own = (hcol // HEAD_DIM) == hrow
        q = q_ref[0].astype(F32)
        qbd = [jnp.where(own, jnp.broadcast_to(q[t:t + 1], (N_HEADS, D_MODEL)), 0.0) for t in range(nt)]
        qbd_ref[...] = jnp.concatenate(qbd, axis=0).astype(BF16)

        key_row = lax.broadcasted_iota(jnp.int32, (PAGE, D_MODEL), 0)
        key_row_h = lax.broadcasted_iota(jnp.int32, (PAGE, N_HEADS), 0)
        kn, vn, lfn = kn_ref[0], vn_ref[0], lfn_ref[0]
        kpad = jnp.zeros((PAGE, D_MODEL), F32)
        vpad = jnp.zeros((PAGE, D_MODEL), F32)
        ypad = jnp.zeros((PAGE, N_HEADS), F32)
        cn = jnp.zeros((1, N_HEADS), F32)
        for t in range(nt):
            cn = cn + lfn[t:t + 1]
            kpad = jnp.where(key_row == t, jnp.broadcast_to(kn[t:t + 1], (PAGE, D_MODEL)), kpad)
            vpad = jnp.where(key_row == t, jnp.broadcast_to(vn[t:t + 1], (PAGE, D_MODEL)), vpad)
            ypad = jnp.where(key_row_h == t, jnp.broadcast_to(-cn, (PAGE, N_HEADS)), ypad)
        s = lax.dot_general(qbd_ref[...], kpad.astype(BF16), NT_DIMS, preferred_element_type=F32)
        for part in _split3(ypad):
            s = s + lax.dot_general(sel, part, NT_DIMS, preferred_element_type=F32)
        srow = lax.broadcasted_iota(jnp.int32, (rows, PAGE), 0)
        scol = lax.broadcasted_iota(jnp.int32, (rows, PAGE), 1)
        s = jnp.where(scol <= (srow // N_HEADS), s, -jnp.inf)
        m = jnp.max(s, axis=1, keepdims=True)
        pr = jnp.exp(s - m)
        m_ref[...] = m
        l_ref[...] = jnp.sum(pr, axis=1, keepdims=True)
        acc_ref[...] = jnp.dot(pr.astype(BF16), vpad.astype(BF16), preferred_element_type=F32)

    for cp in page_copies(b, j, slot):
        cp.wait()

    qbd = qbd_ref[...]
    tiles = []
    for g in range(group):
        sg = jnp.dot(qbd, kbuf[slot, g].astype(BF16), preferred_element_type=F32)
        tiles.append(sg + jnp.concatenate([yb_ref[0, g]] * nt, axis=0))
    s = jnp.concatenate(tiles, axis=1)
    m = m_ref[...]
    m_new = jnp.maximum(m, jnp.max(s, axis=1, keepdims=True))
    corr = jnp.exp(m - m_new)
    pr = jnp.exp(s - m_new)
    m_ref[...] = m_new
    l_ref[...] = l_ref[...] * corr + jnp.sum(pr, axis=1, keepdims=True)
    pb = pr.astype(BF16)
    pv = jnp.zeros((rows, D_MODEL), F32)
    for g in range(group):
        pv = pv + lax.dot_general(pb[:, g * PAGE:(g + 1) * PAGE], vbuf[slot, g].astype(BF16), NT_DIMS,
                                  preferred_element_type=F32)
    acc_ref[...] = acc_ref[...] * corr + pv

    @pl.when(j == n_steps - 1)
    def _():
        hrow = lax.broadcasted_iota(jnp.int32, (N_HEADS, D_MODEL), 0)
        hcol = lax.broadcasted_iota(jnp.int32, (N_HEADS, D_MODEL), 1)
        own = (hcol // HEAD_DIM) == hrow
        o = acc_ref[...] / l_ref[...]
        for t in range(nt):
            blk = jnp.where(own, o[t * N_HEADS:(t + 1) * N_HEADS], 0.0)
            o_ref[0, t:t + 1, :] = jnp.sum(blk, axis=0, keepdims=True)


def _decode_attn(page_table, q, k_new, v_new, lf_new, cache_k_l, cache_v_l, ybias):
    nb, n_pages = page_table.shape
    nt = q.shape[1]
    rows = nt * N_HEADS
    n_pool = cache_k_l.shape[0]
    kc = jnp.transpose(cache_k_l, (0, 2, 3, 1)).reshape(n_pool, D_MODEL, PAGE)
    vc = jnp.transpose(cache_v_l, (0, 2, 3, 1)).reshape(n_pool, D_MODEL, PAGE)

    group = math.gcd(n_pages, DECODE_GROUP)
    n_steps = n_pages // group
    per_b = lambda b, j, pt: (b, 0, 0)
    return pl.pallas_call(
        functools.partial(_decode_kernel, nt=nt, n_pages=n_pages, group=group),
        grid_spec=pltpu.PrefetchScalarGridSpec(
            num_scalar_prefetch=1,
            grid=(nb, n_steps),
            in_specs=[pl.BlockSpec((1, nt, D_MODEL), per_b),
                      pl.BlockSpec((1, nt, D_MODEL), per_b),
                      pl.BlockSpec((1, nt, D_MODEL), per_b),
                      pl.BlockSpec((1, nt, N_HEADS), per_b),
                      pl.BlockSpec(memory_space=pl.ANY),
                      pl.BlockSpec(memory_space=pl.ANY),
                      pl.BlockSpec((1, group, N_HEADS, PAGE), lambda b, j, pt: (b, n_steps - 1 - j, 0, 0))],
            out_specs=pl.BlockSpec((1, nt, D_MODEL), per_b),
            scratch_shapes=[pltpu.VMEM((2, group, D_MODEL, PAGE), F32),
                            pltpu.VMEM((2, group, D_MODEL, PAGE), F32),
                            pltpu.SemaphoreType.DMA((2,)),
                            pltpu.VMEM((rows, D_MODEL), BF16),
                            pltpu.VMEM((rows, 1), F32),
                            pltpu.VMEM((rows, 1), F32),
                            pltpu.VMEM((rows, D_MODEL), F32)]),
        out_shape=jax.ShapeDtypeStruct((nb, nt, D_MODEL), F32),
        compiler_params=_cparams(("arbitrary", "arbitrary")),
        name="decode_attn",
    )(page_table.reshape(-1), q, k_new, v_new, lf_new, kc, vc, ybias)


def _row_tile(m):
    return 512 if m % 512 == 0 else m


def kernel(x_prompt, x_sample, cache_k, cache_v, cache_logf, page_table, state_s5_re, state_s5_im,
           attn_w_in, attn_b_f, attn_w_out, s5_w_in, s5_a_re, s5_a_im, s5_log_dt,
           s5_b_re, s5_b_im, s5_c_re, s5_c_im, s5_d, s5_w_glu, s5_b_glu, s5_w_out,
           ln_g, ln_b):
    assert x_prompt.shape[0] == 1
    l = x_prompt.shape[1]
    nb, nt = x_sample.shape[:2]
    width = N_HEADS * HEAD_DIM

    w_in = attn_w_in[0]
    w_qkvz = w_in[:, :4 * width].astype(BF16)
    w_qvz = jnp.concatenate([w_in[:, 0:width], w_in[:, 2 * width:4 * width]], axis=1).astype(BF16)
    wt_kv = w_in[:, width:3 * width].T.astype(BF16)
    w_f = jnp.pad(w_in[:, 4 * width:], ((0, 0), (0, LANES - N_HEADS))).astype(BF16)
    b_f = jnp.pad(attn_b_f[0], (0, LANES - N_HEADS)).reshape(1, LANES)
    w_ao = attn_w_out[0].astype(BF16)
    w_s5in = s5_w_in[0].astype(BF16)
    w_glu = s5_w_glu[0].astype(BF16)
    b_glu = s5_b_glu[0].reshape(1, D_MODEL)
    w_so = s5_w_out[0].astype(BF16)
    g0, b0 = ln_g[0].reshape(1, D_MODEL), ln_b[0].reshape(1, D_MODEL)
    g1, b1 = ln_g[1].reshape(1, D_MODEL), ln_b[1].reshape(1, D_MODEL)
    d_skip = s5_d[0].reshape(1, D_MODEL)

    pre, pim, nre, nim, bb_re, bb_im = _s5_params(s5_a_re[0], s5_a_im[0], s5_log_dt[0],
                                                  s5_b_re[0], s5_b_im[0])
    ppow = _slab_table(pre, pim)
    npow = _slab_table(nre, nim)
    bbd, cbd = _s5_weights(bb_re, bb_im, s5_c_re[0], s5_c_im[0])

    def to_slab_state(re, im):
        f = lambda a: a.reshape(a.shape[0], N_SLABS, SLAB_STATE).transpose(1, 0, 2)
        return jnp.concatenate([f(re), f(im)], axis=-1)

    def from_slab_state(h):
        f = lambda a: a.transpose(1, 0, 2).reshape(a.shape[1], N_SLABS * GROUPS_PER_SLAB, S5_STATE)
        return f(h[..., :SLAB_STATE]), f(h[..., SLAB_STATE:])

    xp = x_prompt[0]
    tm = _row_tile(l)
    q, vb, z, kt, ktb, vt, lf, kn2 = _attn_in_proj_prompt(xp, w_qvz, wt_kv, w_f, b_f, tm)
    ct = _cumsum_t(lf)
    o = _attn_prompt(q, ktb, vb, ct, kn2)
    x1 = _attn_out(o, z, xp, w_ao, g0, b0, tm)
    u, z1 = _s5_in_proj(x1, w_s5in, tm)
    zero_state = jnp.zeros((N_SLABS, 1, 2 * SLAB_STATE), F32)
    y, h_last = _s5_scan(u, bbd, cbd, npow, ppow, d_skip, zero_state)
    y_prompt = _s5_out(y, z1, x1, w_glu, b_glu, w_so, g1, b1, tm)
    hp_re, hp_im = from_slab_state(h_last)

    ms = nb * nt
    xs = x_sample.reshape(ms, D_MODEL)
    qs, ks, vs, _, _, zs, lfs = _attn_in_proj(xs, w_qkvz, w_f, b_f, ms)
    ybias = _past_bias(page_table, cache_logf[0])
    os_ = _decode_attn(page_table, qs.reshape(nb, nt, D_MODEL), ks.reshape(nb, nt, D_MODEL),
                       vs.reshape(nb, nt, D_MODEL), lfs.reshape(nb, nt, N_HEADS),
                       cache_k[0], cache_v[0], ybias)
    xs1 = _attn_out(os_.reshape(ms, D_MODEL), zs, xs, w_ao, g0, b0, ms)
    us, zs1 = _s5_in_proj(xs1, w_s5in, ms)
    tmaj = lambda a: a.reshape(nb, nt, D_MODEL).transpose(1, 0, 2).reshape(ms, D_MODEL)
    bmaj = lambda a: a.reshape(nt, nb, D_MODEL).transpose(1, 0, 2).reshape(ms, D_MODEL)
    h0s = to_slab_state(state_s5_re[0], state_s5_im[0])
    ys_tm, hs_last = _s5_steps(tmaj(us), bbd, cbd, ppow, d_skip, h0s, nb, nt)
    y_sample = _s5_out(bmaj(ys_tm), zs1, xs1, w_glu, b_glu, w_so, g1, b1, ms)
    hs_re, hs_im = from_slab_state(hs_last)

    kv5 = lambda a, bsz, t: a.reshape(1, bsz, t, N_HEADS, HEAD_DIM)
    from_t = lambda a: a.reshape(N_HEADS, HEAD_DIM, l).transpose(2, 0, 1)[None, None]
    return (y_prompt.reshape(1, l, D_MODEL),
            y_sample.reshape(nb, nt, D_MODEL),
            from_t(kt), from_t(vt), lf.reshape(1, 1, l, N_HEADS),
            kv5(ks, nb, nt), kv5(vs, nb, nt), lfs.reshape(1, nb, nt, N_HEADS),
            hp_re[None], hp_im[None], hs_re[None], hs_im[None])
```

```python
import functools
import math

import jax
import jax.numpy as jnp
from jax import lax
from jax.experimental import pallas as pl
from jax.experimental.pallas import tpu as pltpu

F32 = jnp.float32
BF16 = jnp.bfloat16

D_MODEL = 1024
N_HEADS = 16
HEAD_DIM = 64
PAGE = 128
DEPTH = 2
ALPHA = (2 * DEPTH) ** 0.25
LN_EPS = 1e-5
QK_SCALE = HEAD_DIM ** -0.5
LOG2E = math.log2(math.e)
DEAD_GAP = 152.0
BOUND_SLACK = 2.0

LANES = 128
N_SLABS = D_MODEL // LANES
S5_GROUP = 16
S5_STATE = 64
GROUPS_PER_SLAB = LANES // S5_GROUP
SLAB_STATE = GROUPS_PER_SLAB * S5_STATE
S5_CHUNK = 64
DECODE_GROUP = 8
VMEM_LIMIT = 56 * 1024 * 1024

NT_DIMS = (((1,), (1,)), ((), ()))


def _cparams(sem):
    return pltpu.CompilerParams(dimension_semantics=sem, vmem_limit_bytes=VMEM_LIMIT)


def _split2(x):
    hi = x.astype(BF16)
    lo = (x - hi.astype(F32)).astype(BF16)
    return hi, lo


def _split3(x):
    hi = x.astype(BF16)
    r = x - hi.astype(F32)
    mid = r.astype(BF16)
    lo = (r - mid.astype(F32)).astype(BF16)
    return hi, mid, lo


def _layer_norm(r, g, b):
    mu = jnp.mean(r, axis=-1, keepdims=True)
    c = r - mu
    var = jnp.mean(c * c, axis=-1, keepdims=True)
    return c * lax.rsqrt(var + LN_EPS) * g + b


def _silu(z):
    return z * jax.nn.sigmoid(z)


def _attn_in_kernel(x_ref, w_ref, wf_ref, bf_ref, q_ref, k_ref, v_ref, kb_ref, vb_ref, z_ref, lf_ref):
    xb = x_ref[...].astype(BF16)
    q = jnp.dot(xb, w_ref[:, 0:D_MODEL], preferred_element_type=F32)
    q_ref[...] = (q * QK_SCALE).astype(BF16)
    k = jnp.dot(xb, w_ref[:, D_MODEL:2 * D_MODEL], preferred_element_type=F32)
    k_ref[...] = k
    kb_ref[...] = k.astype(BF16)
    v = jnp.dot(xb, w_ref[:, 2 * D_MODEL:3 * D_MODEL], preferred_element_type=F32)
    v_ref[...] = v
    vb_ref[...] = v.astype(BF16)
    z_ref[...] = jnp.dot(xb, w_ref[:, 3 * D_MODEL:4 * D_MODEL], preferred_element_type=F32)
    hf = jnp.dot(xb, wf_ref[...], preferred_element_type=F32) + bf_ref[...]
    lf = jnp.minimum(hf, 0.0) - jnp.log(1.0 + jnp.exp(-jnp.abs(hf)))
    lf_ref[...] = lf[:, 0:N_HEADS]


def _attn_in_proj(x, w_qkvz, w_f, b_f, tm):
    m = x.shape[0]
    row = lambda i: (i, 0)
    const = lambda i: (0, 0)
    wide = pl.BlockSpec((tm, D_MODEL), row)
    return pl.pallas_call(
        _attn_in_kernel,
        grid=(m // tm,),
        in_specs=[wide,
                  pl.BlockSpec((D_MODEL, 4 * D_MODEL), const),
                  pl.BlockSpec((D_MODEL, LANES), const),
                  pl.BlockSpec((1, LANES), const)],
        out_specs=[wide, wide, wide, wide, wide, wide, pl.BlockSpec((tm, N_HEADS), row)],
        out_shape=[jax.ShapeDtypeStruct((m, D_MODEL), BF16),
                   jax.ShapeDtypeStruct((m, D_MODEL), F32),
                   jax.ShapeDtypeStruct((m, D_MODEL), F32),
                   jax.ShapeDtypeStruct((m, D_MODEL), BF16),
                   jax.ShapeDtypeStruct((m, D_MODEL), BF16),
                   jax.ShapeDtypeStruct((m, D_MODEL), F32),
                   jax.ShapeDtypeStruct((m, N_HEADS), F32)],
        compiler_params=_cparams(("arbitrary",)),
        name="attn_in_proj",
    )(x, w_qkvz, w_f, b_f)


def _attn_in_prompt_kernel(x_ref, w_ref, wt_ref, wf_ref, bf_ref,
                           q_ref, vb_ref, z_ref, kt_ref, ktb_ref, vt_ref, lf_ref, kn_ref):
    xb = x_ref[...].astype(BF16)
    q = jnp.dot(xb, w_ref[:, 0:D_MODEL], preferred_element_type=F32)
    q_ref[...] = (q * (QK_SCALE * LOG2E)).astype(BF16)
    vb_ref[...] = jnp.dot(xb, w_ref[:, D_MODEL:2 * D_MODEL], preferred_element_type=F32).astype(BF16)
    z_ref[...] = jnp.dot(xb, w_ref[:, 2 * D_MODEL:3 * D_MODEL], preferred_element_type=F32)
    kt = lax.dot_general(wt_ref[0:D_MODEL, :], xb, NT_DIMS, preferred_element_type=F32)
    kt_ref[...] = kt
    ktb = kt.astype(BF16)
    ktb_ref[...] = ktb
    kf = ktb.astype(F32)
    norm2 = jnp.sum((kf * kf).reshape(N_HEADS, HEAD_DIM, kf.shape[1]), axis=1)
    blockmax = jnp.max(norm2, axis=1, keepdims=True)

    @pl.when(pl.program_id(0) == 0)
    def _():
        kn_ref[...] = jnp.zeros_like(kn_ref)

    kn_ref[...] = jnp.maximum(kn_ref[...], blockmax)
    vt_ref[...] = lax.dot_general(wt_ref[D_MODEL:2 * D_MODEL, :], xb, NT_DIMS, preferred_element_type=F32)
    hf = jnp.dot(xb, wf_ref[...], preferred_element_type=F32) + bf_ref[...]
    lf = jnp.minimum(hf, 0.0) - jnp.log(1.0 + jnp.exp(-jnp.abs(hf)))
    lf_ref[...] = lf[:, 0:N_HEADS]


def _attn_in_proj_prompt(x, w_qvz, wt_kv, w_f, b_f, tm):
    m = x.shape[0]
    row = lambda i: (i, 0)
    col = lambda i: (0, i)
    const = lambda i: (0, 0)
    wide = pl.BlockSpec((tm, D_MODEL), row)
    tall = pl.BlockSpec((D_MODEL, tm), col)
    return pl.pallas_call(
        _attn_in_prompt_kernel,
        grid=(m // tm,),
        in_specs=[wide,
                  pl.BlockSpec((D_MODEL, 3 * D_MODEL), const),
                  pl.BlockSpec((2 * D_MODEL, D_MODEL), const),
                  pl.BlockSpec((D_MODEL, LANES), const),
                  pl.BlockSpec((1, LANES), const)],
        out_specs=[wide, wide, wide, tall, tall, tall, pl.BlockSpec((tm, N_HEADS), row),
                   pl.BlockSpec((N_HEADS, LANES), const)],
        out_shape=[jax.ShapeDtypeStruct((m, D_MODEL), BF16),
                   jax.ShapeDtypeStruct((m, D_MODEL), BF16),
                   jax.ShapeDtypeStruct((m, D_MODEL), F32),
                   jax.ShapeDtypeStruct((D_MODEL, m), F32),
                   jax.ShapeDtypeStruct((D_MODEL, m), BF16),
                   jax.ShapeDtypeStruct((D_MODEL, m), F32),
                   jax.ShapeDtypeStruct((m, N_HEADS), F32),
                   jax.ShapeDtypeStruct((N_HEADS, LANES), F32)],
        compiler_params=_cparams(("arbitrary",)),
        name="attn_in_proj_prompt",
    )(x, w_qvz, wt_kv, w_f, b_f)


def _cumsum_kernel(lf_ref, ct_ref, carry_ref, *, tc):
    @pl.when(pl.program_id(0) == 0)
    def _():
        carry_ref[...] = jnp.zeros_like(carry_ref)

    r = lax.broadcasted_iota(jnp.int32, (tc, tc), 0)
    c = lax.broadcasted_iota(jnp.int32, (tc, tc), 1)
    tri = jnp.where(r >= c, 1.0, 0.0).astype(BF16)
    acc = carry_ref[...]
    for part in _split3(lf_ref[...]):
        acc = acc + jnp.dot(tri, part, preferred_element_type=F32)
    carry_ref[...] = acc[tc - 1:tc, :]
    hr = lax.broadcasted_iota(jnp.int32, (N_HEADS, N_HEADS), 0)
    hc = lax.broadcasted_iota(jnp.int32, (N_HEADS, N_HEADS), 1)
    eye = jnp.where(hr == hc, 1.0, 0.0).astype(BF16)
    out = jnp.zeros((N_HEADS, tc), F32)
    for part in _split3(acc):
        out = out + lax.dot_general(eye, part, NT_DIMS, preferred_element_type=F32)
    ct_ref[...] = out


def _cumsum_t(logf, tc=256):
    l = logf.shape[0]
    return pl.pallas_call(
        functools.partial(_cumsum_kernel, tc=tc),
        grid=(l // tc,),
        in_specs=[pl.BlockSpec((tc, N_HEADS), lambda i: (i, 0))],
        out_specs=pl.BlockSpec((N_HEADS, tc), lambda i: (0, i)),
        out_shape=jax.ShapeDtypeStruct((N_HEADS, l), F32),
        scratch_shapes=[pltpu.VMEM((1, N_HEADS), F32)],
        compiler_params=_cparams(("arbitrary",)),
        name="logf_cumsum",
    )(logf)


def _attn_prompt_kernel(q_ref, kt_ref, v_ref, ct_ref, kn_ref, o_ref, s0_ref, s1_ref, m_ref, acc_ref, *, tq):
    p = pl.program_id(0)
    qi = pl.program_id(1)
    q = q_ref[...]
    lane = lax.broadcasted_iota(jnp.int32, (1, LANES), 1)
    first = lane < HEAD_DIM
    zero = jnp.zeros_like(q)
    qs = (jnp.where(first, q, zero), jnp.where(first, zero, q))
    ones_col = (jnp.where(lane == HEAD_DIM, 1.0, 0.0).astype(BF16), jnp.where(lane == 0, 1.0, 0.0).astype(BF16))
    q0 = pl.multiple_of(qi * tq, tq)
    c0 = [ct_ref[pl.ds(2 * p + e, 1), pl.ds(q0, tq)][:, 0:1] for e in range(2)]
    row = lax.broadcasted_iota(jnp.int32, (tq, tq), 0)
    col = lax.broadcasted_iota(jnp.int32, (tq, tq), 1)
    causal = col <= row

    def scores(j, dst_ref):
        k0 = pl.multiple_of(j * tq, tq)
        kt = kt_ref[:, pl.ds(k0, tq)]
        for e in range(2):
            bias = (c0[e] - ct_ref[pl.ds(2 * p + e, 1), pl.ds(k0, tq)]) * LOG2E
            dst_ref[e] = jnp.dot(qs[e], kt, preferred_element_type=F32) + bias

    def absorb(j, src_ref, masked):
        k0 = pl.multiple_of(j * tq, tq)
        v = v_ref[pl.ds(k0, tq), :]
        vs = (jnp.where(first, v, ones_col[0]), jnp.where(first, ones_col[1], v))
        for e in range(2):
            s = src_ref[e]
            if masked:
                s = jnp.where(causal, s, -jnp.inf)
            m = m_ref[e]
            m_new = jnp.maximum(m, jnp.max(s, axis=1, keepdims=True))
            pr = jnp.exp2(s - m_new).astype(BF16)
            acc_ref[e] = jnp.exp2(m - m_new) * acc_ref[e] + jnp.dot(pr, vs[e], preferred_element_type=F32)
            m_ref[e] = m_new

    m_ref[...] = jnp.full(m_ref.shape, -jnp.inf, F32)
    acc_ref[...] = jnp.zeros(acc_ref.shape, F32)
    scores(qi, s0_ref)
    scores(jnp.maximum(qi - 1, 0), s1_ref)
    absorb(qi, s0_ref, True)

    pos = lax.broadcasted_iota(jnp.int32, (1, ct_ref.shape[1]), 1)
    n_blocks = 0
    for e in range(2):
        m_min = jnp.min(m_ref[e], axis=0, keepdims=True)
        qf = qs[e].astype(F32)
        qn2 = jnp.max(jnp.sum(qf * qf, axis=1, keepdims=True), axis=0, keepdims=True)
        kn2 = kn_ref[pl.ds(2 * p + e, 1), :][:, 0:1]
        reach = jnp.sqrt(qn2 * kn2) + BOUND_SLACK
        upper = (c0[e] - ct_ref[pl.ds(2 * p + e, 1), :]) * LOG2E + reach
        live = jnp.logical_and(upper >= m_min - DEAD_GAP, pos < q0)
        count = jnp.sum(jnp.where(live, 1.0, 0.0)).astype(jnp.int32)
        n_blocks = jnp.maximum(n_blocks, (count + (tq - 1)) // tq)

    def visit(t, _):
        j = qi - t
        nxt = jnp.maximum(j - 1, 0)

        @pl.when(t % 2 == 1)
        def _():
            scores(nxt, s0_ref)
            absorb(j, s1_ref, False)

        @pl.when(t % 2 == 0)
        def _():
            scores(nxt, s1_ref)
            absorb(j, s0_ref, False)

        return 0

    lax.fori_loop(1, n_blocks + 1, visit, 0)
    acca, accb = acc_ref[0], acc_ref[1]
    o_ref[...] = jnp.where(first, acca / acca[:, HEAD_DIM:HEAD_DIM + 1], accb / accb[:, 0:1])


def _attn_prompt(q, ktb, vb, ct, kn2, tq=512):
    l = q.shape[0]
    return pl.pallas_call(
        functools.partial(_attn_prompt_kernel, tq=tq),
        grid=(N_SLABS, l // tq),
        in_specs=[pl.BlockSpec((tq, LANES), lambda p, i: (i, p)),
                  pl.BlockSpec((LANES, l), lambda p, i: (p, 0)),
                  pl.BlockSpec((l, LANES), lambda p, i: (0, p)),
                  pl.BlockSpec((N_HEADS, l), lambda p, i: (0, 0)),
                  pl.BlockSpec((N_HEADS, LANES), lambda p, i: (0, 0))],
        out_specs=pl.BlockSpec((tq, LANES), lambda p, i: (i, p)),
        out_shape=jax.ShapeDtypeStruct((l, D_MODEL), F32),
        scratch_shapes=[pltpu.VMEM((2, tq, tq), F32), pltpu.VMEM((2, tq, tq), F32),
                        pltpu.VMEM((2, tq, 1), F32), pltpu.VMEM((2, tq, LANES), F32)],
        compiler_params=_cparams(("arbitrary", "arbitrary")),
        name="attn_prompt",
    )(q, ktb, vb, ct, kn2)


def _attn_out_kernel(o_ref, z_ref, x_ref, w_ref, g_ref, b_ref, y_ref):
    a = (o_ref[...] * _silu(z_ref[...])).astype(BF16)
    sub = jnp.dot(a, w_ref[...], preferred_element_type=F32)
    y_ref[...] = _layer_norm(ALPHA * x_ref[...] + sub, g_ref[...], b_ref[...])


def _attn_out(o, z, x, w_out, g, b, tm):
    m = x.shape[0]
    wide = pl.BlockSpec((tm, D_MODEL), lambda i: (i, 0))
    const = lambda i: (0, 0)
    return pl.pallas_call(
        _attn_out_kernel,
        grid=(m // tm,),
        in_specs=[wide, wide, wide, pl.BlockSpec((D_MODEL, D_MODEL), const),
                  pl.BlockSpec((1, D_MODEL), const), pl.BlockSpec((1, D_MODEL), const)],
        out_specs=wide,
        out_shape=jax.ShapeDtypeStruct((m, D_MODEL), F32),
        compiler_params=_cparams(("arbitrary",)),
        name="attn_out_ln",
    )(o, z, x, w_out, g, b)


def _s5_in_kernel(x_ref, w_ref, u_ref, z_ref):
    xb = x_ref[...].astype(BF16)
    u_ref[...] = jnp.dot(xb, w_ref[:, 0:D_MODEL], preferred_element_type=F32)
    z_ref[...] = jnp.dot(xb, w_ref[:, D_MODEL:2 * D_MODEL], preferred_element_type=F32)


def _s5_in_proj(x, w_in, tm):
    m = x.shape[0]
    wide = pl.BlockSpec((tm, D_MODEL), lambda i: (i, 0))
    return pl.pallas_call(
        _s5_in_kernel,
        grid=(m // tm,),
        in_specs=[wide, pl.BlockSpec((D_MODEL, 2 * D_MODEL), lambda i: (0, 0))],
        out_specs=[wide, wide],
        out_shape=[jax.ShapeDtypeStruct((m, D_MODEL), F32)] * 2,
        compiler_params=_cparams(("arbitrary",)),
        name="s5_in_proj",
    )(x, w_in)


def _s5_param_kernel(are_ref, aim_ref, ldt_ref, arer_ref, aimr_ref, ldtr_ref, bre_ref, bim_ref,
                     pre_ref, pim_ref, nre_ref, nim_ref, bbre_ref, bbim_ref):
    def abar(are, aim, ldt):
        dt = jnp.exp(ldt)
        mag = jnp.exp(are * dt)
        ang = aim * dt
        return mag * jnp.cos(ang), mag * jnp.sin(ang)

    are, aim = are_ref[...], aim_ref[...]
    ar, ai = abar(are, aim, ldt_ref[...])
    inv = 1.0 / (ar * ar + ai * ai)
    nr, ni = ar * inv, -ai * inv
    pr, pi = jnp.ones_like(ar), jnp.zeros_like(ar)
    qr, qi = pr, pi
    for t in range(S5_CHUNK):
        pre_ref[t] = pr
        pim_ref[t] = pi
        nre_ref[t] = qr
        nim_ref[t] = qi
        pr, pi = pr * ar - pi * ai, pr * ai + pi * ar
        qr, qi = qr * nr - qi * ni, qr * ni + qi * nr

    lr, li = arer_ref[...], aimr_ref[...]
    er, ei = abar(lr, li, ldtr_ref[...])
    linv = 1.0 / (lr * lr + li * li)
    fr = ((er - 1.0) * lr + ei * li) * linv
    fi = (ei * lr - (er - 1.0) * li) * linv
    br, bi = bre_ref[...], bim_ref[...]
    bbre_ref[...] = fr * br - fi * bi
    bbim_ref[...] = fr * bi + fi * br


def _s5_params(a_re, a_im, log_dt, b_re, b_im):
    g, p = a_re.shape
    rep = lambda a: jnp.repeat(a, S5_GROUP, axis=0)
    ldt = jnp.broadcast_to(log_dt[:, None], (g, p))
    b_re_t = b_re.transpose(0, 2, 1).reshape(g * S5_GROUP, p)
    b_im_t = b_im.transpose(0, 2, 1).reshape(g * S5_GROUP, p)
    tab = jax.ShapeDtypeStruct((S5_CHUNK, g, p), F32)
    big = jax.ShapeDtypeStruct((g * S5_GROUP, p), F32)
    return pl.pallas_call(
        _s5_param_kernel,
        out_shape=[tab, tab, tab, tab, big, big],
        name="s5_discretise",
    )(a_re, a_im, ldt, rep(a_re), rep(a_im), rep(ldt), b_re_t, b_im_t)


def _slab_table(t_re, t_im):
    t = t_re.shape[0]
    f = lambda a: a.reshape(t, N_SLABS, SLAB_STATE).transpose(1, 0, 2)
    return jnp.concatenate([f(t_re), f(t_im)], axis=-1)


def _s5_weights(bb_re, bb_im, c_re, c_im):
    eye = jnp.eye(GROUPS_PER_SLAB, dtype=F32)
    shape_b = (N_SLABS, GROUPS_PER_SLAB, S5_GROUP, S5_STATE)

    def in_mat(b):
        return jnp.einsum('kgjp,gh->kgjhp', b.reshape(shape_b), eye).reshape(N_SLABS, LANES, SLAB_STATE)

    def out_mat(c):
        return jnp.einsum('kgip,gh->kgphi', c.reshape(shape_b), eye).reshape(N_SLABS, SLAB_STATE, LANES)

    bbd = jnp.concatenate([in_mat(bb_re), in_mat(bb_im)], axis=-1).astype(BF16)
    cbd = jnp.concatenate([out_mat(c_re), -out_mat(c_im)], axis=1).astype(BF16)
    return bbd, cbd


def _cmul(ar, ai, br, bi):
    return ar * br - ai * bi, ar * bi + ai * br


def _s5_scan_kernel(u_ref, bbd_ref, cbd_ref, np_ref, pp_ref, d_ref, h0_ref, y_ref, hl_ref, hc_ref,
                    *, tr):
    t = S5_CHUNK
    half = SLAB_STATE

    @pl.when(pl.program_id(0) == 0)
    def _():
        hc_ref[...] = h0_ref[...]

    r = lax.broadcasted_iota(jnp.int32, (t, t), 0)
    c = lax.broadcasted_iota(jnp.int32, (t, t), 1)
    tri = jnp.where(r >= c, 1.0, 0.0).astype(BF16)

    for k in range(N_SLABS):
        cols = slice(k * LANES, (k + 1) * LANES)
        uk = u_ref[:, cols]
        bu = jnp.dot(uk.astype(BF16), bbd_ref[k], preferred_element_type=F32)
        nr, ni = np_ref[k, :, 0:half], np_ref[k, :, half:]
        pr, pi = pp_ref[k, :, 0:half], pp_ref[k, :, half:]
        ar, ai = pp_ref[k, 1:2, 0:half], pp_ref[k, 1:2, half:]
        hp = hc_ref[k]
        hpr, hpi = hp[:, 0:half], hp[:, half:]
        hh = []
        for ci in range(tr // t):
            rows = slice(ci * t, (ci + 1) * t)
            zr, zi = _cmul(nr, ni, bu[rows, 0:half], bu[rows, half:])
            zz = jnp.concatenate([zr, zi], axis=1).astype(BF16)
            s = jnp.dot(tri, zz, preferred_element_type=F32)
            gr, gi = _cmul(ar, ai, hpr, hpi)
            hr, hi = _cmul(pr, pi, s[:, 0:half] + gr, s[:, half:] + gi)
            hpr, hpi = hr[t - 1:t], hi[t - 1:t]
            hh.append(jnp.concatenate([hr, hi], axis=1).astype(BF16))
        hc_ref[k] = jnp.concatenate([hpr, hpi], axis=1)
        y = jnp.dot(jnp.concatenate(hh, axis=0), cbd_ref[k], preferred_element_type=F32)
        y_ref[:, cols] = y + d_ref[:, cols] * uk
    hl_ref[...] = hc_ref[...]


def _s5_scan(u, bbd, cbd, npow, ppow, d, h0, tr=512):
    l = u.shape[0]
    c3 = lambda i: (0, 0, 0)
    state = pl.BlockSpec((N_SLABS, 1, 2 * SLAB_STATE), c3)
    tabspec = pl.BlockSpec((N_SLABS, S5_CHUNK, 2 * SLAB_STATE), c3)
    return pl.pallas_call(
        functools.partial(_s5_scan_kernel, tr=tr),
        grid=(l // tr,),
        in_specs=[pl.BlockSpec((tr, D_MODEL), lambda i: (i, 0)),
                  pl.BlockSpec((N_SLABS, LANES, 2 * SLAB_STATE), c3),
                  pl.BlockSpec((N_SLABS, 2 * SLAB_STATE, LANES), c3),
                  tabspec, tabspec,
                  pl.BlockSpec((1, D_MODEL), lambda i: (0, 0)),
                  state],
        out_specs=[pl.BlockSpec((tr, D_MODEL), lambda i: (i, 0)), state],
        out_shape=[jax.ShapeDtypeStruct((l, D_MODEL), F32),
                   jax.ShapeDtypeStruct((N_SLABS, 1, 2 * SLAB_STATE), F32)],
        scratch_shapes=[pltpu.VMEM((N_SLABS, 1, 2 * SLAB_STATE), F32)],
        compiler_params=_cparams(("arbitrary",)),
        name="s5_scan_prompt",
    )(u, bbd, cbd, npow, ppow, d, h0)


def _s5_step_kernel(u_ref, bbd_ref, cbd_ref, pp_ref, d_ref, h0_ref, y_ref, hl_ref, *, nb, nt):
    half = SLAB_STATE
    for k in range(N_SLABS):
        cols = slice(k * LANES, (k + 1) * LANES)
        uk = u_ref[:, cols]
        bu = jnp.dot(uk.astype(BF16), bbd_ref[k], preferred_element_type=F32)
        ar, ai = pp_ref[k, 1:2, 0:half], pp_ref[k, 1:2, half:]
        h = h0_ref[k]
        hr, hi = h[:, 0:half], h[:, half:]
        hs = []
        for t in range(nt):
            rows = slice(t * nb, (t + 1) * nb)
            gr, gi = _cmul(ar, ai, hr, hi)
            hr, hi = gr + bu[rows, 0:half], gi + bu[rows, half:]
            hs.append(jnp.concatenate([hr, hi], axis=1))
        hl_ref[k] = hs[-1]
        hh = jnp.concatenate(hs, axis=0).astype(BF16)
        y_ref[:, cols] = jnp.dot(hh, cbd_ref[k], preferred_element_type=F32) + d_ref[:, cols] * uk


def _s5_steps(u_tm, bbd, cbd, ppow, d, h0, nb, nt):
    return pl.pallas_call(
        functools.partial(_s5_step_kernel, nb=nb, nt=nt),
        out_shape=[jax.ShapeDtypeStruct((nt * nb, D_MODEL), F32),
                   jax.ShapeDtypeStruct((N_SLABS, nb, 2 * SLAB_STATE), F32)],
        compiler_params=pltpu.CompilerParams(vmem_limit_bytes=VMEM_LIMIT),
        name="s5_scan_sample",
    )(u_tm, bbd, cbd, ppow, d, h0)


def _s5_out_kernel(y_ref, z_ref, x_ref, wg_ref, bg_ref, wo_ref, g_ref, b_ref, o_ref):
    g = jax.nn.gelu(y_ref[...])
    gate = jax.nn.sigmoid(jnp.dot(g.astype(BF16), wg_ref[...], preferred_element_type=F32) + bg_ref[...])
    a = (g * gate * _silu(z_ref[...])).astype(BF16)
    sub = jnp.dot(a, wo_ref[...], preferred_element_type=F32)
    o_ref[...] = _layer_norm(ALPHA * x_ref[...] + sub, g_ref[...], b_ref[...])


def _s5_out(y, z, x, w_glu, b_glu, w_out, g, b, tm):
    m = x.shape[0]
    wide = pl.BlockSpec((tm, D_MODEL), lambda i: (i, 0))
    const = lambda i: (0, 0)
    mat = pl.BlockSpec((D_MODEL, D_MODEL), const)
    vec = pl.BlockSpec((1, D_MODEL), const)
    return pl.pallas_call(
        _s5_out_kernel,
        grid=(m // tm,),
        in_specs=[wide, wide, wide, mat, vec, mat, vec, vec],
        out_specs=wide,
        out_shape=jax.ShapeDtypeStruct((m, D_MODEL), F32),
        compiler_params=_cparams(("arbitrary",)),
        name="s5_out_ln",
    )(y, z, x, w_glu, b_glu, w_out, g, b)


def _shift_pages(a, d):
    return jnp.concatenate([a[d:], jnp.zeros((d,) + a.shape[1:], a.dtype)], axis=0)


def _past_bias_kernel(pt_ref, lf_hbm, y_ref, buf_ref, sem, *, n_pages):
    b = pl.program_id(0)
    slot = b % 2

    def page_copy(bb, sl, i):
        return pltpu.make_async_copy(lf_hbm.at[pt_ref[bb * n_pages + i]], buf_ref.at[sl, i], sem.at[sl])

    def gather(bb, sl):
        def start(i, _):
            page_copy(bb, sl, i).start()
            return 0
        lax.fori_loop(0, n_pages, start, 0)

    @pl.when(b == 0)
    def _():
        gather(0, 0)

    @pl.when(b + 1 < pl.num_programs(0))
    def _():
        gather(b + 1, 1 - slot)

    def wait(i, _):
        page_copy(b, slot, i).wait()
        return 0

    lax.fori_loop(0, n_pages, wait, 0)

    x = buf_ref[slot].reshape(n_pages * N_HEADS, PAGE)
    kr = lax.broadcasted_iota(jnp.int32, (PAGE, 2 * PAGE), 0)
    kc = lax.broadcasted_iota(jnp.int32, (PAGE, 2 * PAGE), 1)
    later_or_all = jnp.where(jnp.logical_or(kc >= PAGE, kr > kc), 1.0, 0.0).astype(BF16)
    both = jnp.zeros((n_pages * N_HEADS, 2 * PAGE), F32)
    for part in _split3(x):
        both = both + jnp.dot(part, later_or_all, preferred_element_type=F32)
    excl = both[:, 0:PAGE].reshape(n_pages, N_HEADS, PAGE)
    after = both[:, PAGE:].reshape(n_pages, N_HEADS, PAGE)
    d = 1
    while d < n_pages:
        after = after + _shift_pages(after, d)
        d *= 2
    y_ref[0] = excl + _shift_pages(after, 1)


def _past_bias(page_table, cache_logf_l):
    nb, n_pages = page_table.shape
    lft = jnp.transpose(cache_logf_l, (0, 2, 1))
    return pl.pallas_call(
        functools.partial(_past_bias_kernel, n_pages=n_pages),
        grid_spec=pltpu.PrefetchScalarGridSpec(
            num_scalar_prefetch=1,
            grid=(nb,),
            in_specs=[pl.BlockSpec(memory_space=pl.ANY)],
            out_specs=pl.BlockSpec((1, n_pages, N_HEADS, PAGE), lambda b, pt: (b, 0, 0, 0)),
            scratch_shapes=[pltpu.VMEM((2, n_pages, N_HEADS, PAGE), F32), pltpu.SemaphoreType.DMA((2,))]),
        out_shape=jax.ShapeDtypeStruct((nb, n_pages, N_HEADS, PAGE), F32),
        compiler_params=_cparams(("arbitrary",)),
        name="past_bias",
    )(page_table.reshape(-1), lft)


def _decode_kernel(pt_ref, q_ref, kn_ref, vn_ref, lfn_ref, kc_hbm, vc_hbm, yb_ref, o_ref,
                   kbuf, vbuf, sem, qbd_ref, m_ref, l_ref, acc_ref, *, nt, n_pages, group):
    b = pl.program_id(0)
    j = pl.program_id(1)
    n_steps = n_pages // group
    step = b * n_steps + j
    slot = step % 2
    rows = nt * N_HEADS
    rsel = lax.broadcasted_iota(jnp.int32, (rows, N_HEADS), 0)
    csel = lax.broadcasted_iota(jnp.int32, (rows, N_HEADS), 1)
    sel = jnp.where((rsel & (N_HEADS - 1)) == csel, 1.0, 0.0).astype(BF16)

    def page_copies(bb, jj, sl):
        out = []
        for g in range(group):
            page = pt_ref[bb * n_pages + n_pages - (jj + 1) * group + g]
            out.append(pltpu.make_async_copy(kc_hbm.at[page], kbuf.at[sl, g], sem.at[sl]))
            out.append(pltpu.make_async_copy(vc_hbm.at[page], vbuf.at[sl, g], sem.at[sl]))
        return out

    @pl.when(step == 0)
    def _():
        for cp in page_copies(0, 0, 0):
            cp.start()

    @pl.when(step + 1 < pl.num_programs(0) * n_steps)
    def _():
        wrap = j + 1 == n_steps
        for cp in page_copies(jnp.where(wrap, b + 1, b), jnp.where(wrap, 0, j + 1), 1 - slot):
            cp.start()

    @pl.when(j == 0)
    def _():
        hrow = lax.broadcasted_iota(jnp.int32, (N_HEADS, D_MODEL), 0)
        hcol = lax.broadcasted_iota(jnp.int32, (N_HEADS, D_MODEL), 1)
        own = (hcol // HEAD_DIM) == hrow
        q = q_ref[0].astype(F32)
        qbd = [jnp.where(own, jnp.broadcast_to(q[t:t + 1], (N_HEADS, D_MODEL)), 0.0) for t in range(nt)]
        qbd_ref[...] = jnp.concatenate(qbd, axis=0).astype(BF16)

        key_row = lax.broadcasted_iota(jnp.int32, (PAGE, D_MODEL), 0)
        key_row_h = lax.broadcasted_iota(jnp.int32, (PAGE, N_HEADS), 0)
        kn, vn, lfn = kn_ref[0], vn_ref[0], lfn_ref[0]
        kpad = jnp.zeros((PAGE, D_MODEL), F32)
        vpad = jnp.zeros((PAGE, D_MODEL), F32)
        ypad = jnp.zeros((PAGE, N_HEADS), F32)
        cn = jnp.zeros((1, N_HEADS), F32)
        for t in range(nt):
            cn = cn + lfn[t:t + 1]
            kpad = jnp.where(key_row == t, jnp.broadcast_to(kn[t:t + 1], (PAGE, D_MODEL)), kpad)
            vpad = jnp.where(key_row == t, jnp.broadcast_to(vn[t:t + 1], (PAGE, D_MODEL)), vpad)
            ypad = jnp.where(key_row_h == t, jnp.broadcast_to(-cn, (PAGE, N_HEADS)), ypad)
        s = lax.dot_general(qbd_ref[...], kpad.astype(BF16), NT_DIMS, preferred_element_type=F32)
        for part in _split3(ypad):
            s = s + lax.dot_general(sel, part, NT_DIMS, preferred_element_type=F32)
        srow = lax.broadcasted_iota(jnp.int32, (rows, PAGE), 0)
        scol = lax.broadcasted_iota(jnp.int32, (rows, PAGE), 1)
        s = jnp.where(scol <= (srow // N_HEADS), s, -jnp.inf)
        m = jnp.max(s, axis=1, keepdims=True)
        pr = jnp.exp(s - m)
        m_ref[...] = m
        l_ref[...] = jnp.sum(pr, axis=1, keepdims=True)
        acc_ref[...] = jnp.dot(pr.astype(BF16), vpad.astype(BF16), preferred_element_type=F32)

    for cp in page_copies(b, j, slot):
        cp.wait()

    qbd = qbd_ref[...]
    tiles = []
    for g in range(group):
        sg = jnp.dot(qbd, kbuf[slot, g].astype(BF16), preferred_element_type=F32)
        tiles.append(sg + jnp.concatenate([yb_ref[0, g]] * nt, axis=0))
    s = jnp.concatenate(tiles, axis=1)
    m = m_ref[...]
    m_new = jnp.maximum(m, jnp.max(s, axis=1, keepdims=True))
    corr = jnp.exp(m - m_new)
    pr = jnp.exp(s - m_new)
    m_ref[...] = m_new
    l_ref[...] = l_ref[...] * corr + jnp.sum(pr, axis=1, keepdims=True)
    pb = pr.astype(BF16)
    pv = jnp.zeros((rows, D_MODEL), F32)
    for g in range(group):
        pv = pv + lax.dot_general(pb[:, g * PAGE:(g + 1) * PAGE], vbuf[slot, g].astype(BF16), NT_DIMS,
                                  preferred_element_type=F32)
    acc_ref[...] = acc_ref[...] * corr + pv

    @pl.when(j == n_steps - 1)
    def _():
        hrow = lax.broadcasted_iota(jnp.int32, (N_HEADS, D_MODEL), 0)
        hcol = lax.broadcasted_iota(jnp.int32, (N_HEADS, D_MODEL), 1)
        own = (hcol // HEAD_DIM) == hrow
        o = acc_ref[...] / l_ref[...]
        for t in range(nt):
            blk = jnp.where(own, o[t * N_HEADS:(t + 1) * N_HEADS], 0.0)
            o_ref[0, t:t + 1, :] = jnp.sum(blk, axis=0, keepdims=True)


def _decode_attn(page_table, q, k_new, v_new, lf_new, cache_k_l, cache_v_l, ybias):
    nb, n_pages = page_table.shape
    nt = q.shape[1]
    rows = nt * N_HEADS
    n_pool = cache_k_l.shape[0]
    kc = jnp.transpose(cache_k_l, (0, 2, 3, 1)).reshape(n_pool, D_MODEL, PAGE)
    vc = jnp.transpose(cache_v_l, (0, 2, 3, 1)).reshape(n_pool, D_MODEL, PAGE)

    group = math.gcd(n_pages, DECODE_GROUP)
    n_steps = n_pages // group
    per_b = lambda b, j, pt: (b, 0, 0)
    return pl.pallas_call(
        functools.partial(_decode_kernel, nt=nt, n_pages=n_pages, group=group),
        grid_spec=pltpu.PrefetchScalarGridSpec(
            num_scalar_prefetch=1,
            grid=(nb, n_steps),
            in_specs=[pl.BlockSpec((1, nt, D_MODEL), per_b),
                      pl.BlockSpec((1, nt, D_MODEL), per_b),
                      pl.BlockSpec((1, nt, D_MODEL), per_b),
                      pl.BlockSpec((1, nt, N_HEADS), per_b),
                      pl.BlockSpec(memory_space=pl.ANY),
                      pl.BlockSpec(memory_space=pl.ANY),
                      pl.BlockSpec((1, group, N_HEADS, PAGE), lambda b, j, pt: (b, n_steps - 1 - j, 0, 0))],
            out_specs=pl.BlockSpec((1, nt, D_MODEL), per_b),
            scratch_shapes=[pltpu.VMEM((2, group, D_MODEL, PAGE), F32),
                            pltpu.VMEM((2, group, D_MODEL, PAGE), F32),
                            pltpu.SemaphoreType.DMA((2,)),
                            pltpu.VMEM((rows, D_MODEL), BF16),
                            pltpu.VMEM((rows, 1), F32),
                            pltpu.VMEM((rows, 1), F32),
                            pltpu.VMEM((rows, D_MODEL), F32)]),
        out_shape=jax.ShapeDtypeStruct((nb, nt, D_MODEL), F32),
        compiler_params=_cparams(("arbitrary", "arbitrary")),
        name="decode_attn",
    )(page_table.reshape(-1), q, k_new, v_new, lf_new, kc, vc, ybias)


def _row_tile(m):
    return 512 if m % 512 == 0 else m


def kernel(x_prompt, x_sample, cache_k, cache_v, cache_logf, page_table, state_s5_re, state_s5_im,
           attn_w_in, attn_b_f, attn_w_out, s5_w_in, s5_a_re, s5_a_im, s5_log_dt,
           s5_b_re, s5_b_im, s5_c_re, s5_c_im, s5_d, s5_w_glu, s5_b_glu, s5_w_out,
           ln_g, ln_b):
    assert x_prompt.shape[0] == 1
    l = x_prompt.shape[1]
    nb, nt = x_sample.shape[:2]
    width = N_HEADS * HEAD_DIM

    w_in = attn_w_in[0]
    w_qkvz = w_in[:, :4 * width].astype(BF16)
    w_qvz = jnp.concatenate([w_in[:, 0:width], w_in[:, 2 * width:4 * width]], axis=1).astype(BF16)
    wt_kv = w_in[:, width:3 * width].T.astype(BF16)
    w_f = jnp.pad(w_in[:, 4 * width:], ((0, 0), (0, LANES - N_HEADS))).astype(BF16)
    b_f = jnp.pad(attn_b_f[0], (0, LANES - N_HEADS)).reshape(1, LANES)
    w_ao = attn_w_out[0].astype(BF16)
    w_s5in = s5_w_in[0].astype(BF16)
    w_glu = s5_w_glu[0].astype(BF16)
    b_glu = s5_b_glu[0].reshape(1, D_MODEL)
    w_so = s5_w_out[0].astype(BF16)
    g0, b0 = ln_g[0].reshape(1, D_MODEL), ln_b[0].reshape(1, D_MODEL)
    g1, b1 = ln_g[1].reshape(1, D_MODEL), ln_b[1].reshape(1, D_MODEL)
    d_skip = s5_d[0].reshape(1, D_MODEL)

    pre, pim, nre, nim, bb_re, bb_im = _s5_params(s5_a_re[0], s5_a_im[0], s5_log_dt[0],
                                                  s5_b_re[0], s5_b_im[0])
    ppow = _slab_table(pre, pim)
    npow = _slab_table(nre, nim)
    bbd, cbd = _s5_weights(bb_re, bb_im, s5_c_re[0], s5_c_im[0])

    def to_slab_state(re, im):
        f = lambda a: a.reshape(a.shape[0], N_SLABS, SLAB_STATE).transpose(1, 0, 2)
        return jnp.concatenate([f(re), f(im)], axis=-1)

    def from_slab_state(h):
        f = lambda a: a.transpose(1, 0, 2).reshape(a.shape[1], N_SLABS * GROUPS_PER_SLAB, S5_STATE)
        return f(h[..., :SLAB_STATE]), f(h[..., SLAB_STATE:])

    xp = x_prompt[0]
    tm = _row_tile(l)
    q, vb, z, kt, ktb, vt, lf, kn2 = _attn_in_proj_prompt(xp, w_qvz, wt_kv, w_f, b_f, tm)
    ct = _cumsum_t(lf)
    o = _attn_prompt(q, ktb, vb, ct, kn2)
    x1 = _attn_out(o, z, xp, w_ao, g0, b0, tm)
    u, z1 = _s5_in_proj(x1, w_s5in, tm)
    zero_state = jnp.zeros((N_SLABS, 1, 2 * SLAB_STATE), F32)
    y, h_last = _s5_scan(u, bbd, cbd, npow, ppow, d_skip, zero_state)
    y_prompt = _s5_out(y, z1, x1, w_glu, b_glu, w_so, g1, b1, tm)
    hp_re, hp_im = from_slab_state(h_last)

    ms = nb * nt
    xs = x_sample.reshape(ms, D_MODEL)
    qs, ks, vs, _, _, zs, lfs = _attn_in_proj(xs, w_qkvz, w_f, b_f, ms)
    ybias = _past_bias(page_table, cache_logf[0])
    os_ = _decode_attn(page_table, qs.reshape(nb, nt, D_MODEL), ks.reshape(nb, nt, D_MODEL),
                       vs.reshape(nb, nt, D_MODEL), lfs.reshape(nb, nt, N_HEADS),
                       cache_k[0], cache_v[0], ybias)
    xs1 = _attn_out(os_.reshape(ms, D_MODEL), zs, xs, w_ao, g0, b0, ms)
    us, zs1 = _s5_in_proj(xs1, w_s5in, ms)
    tmaj = lambda a: a.reshape(nb, nt, D_MODEL).transpose(1, 0, 2).reshape(ms, D_MODEL)
    bmaj = lambda a: a.reshape(nt, nb, D_MODEL).transpose(1, 0, 2).reshape(ms, D_MODEL)
    h0s = to_slab_state(state_s5_re[0], state_s5_im[0])
    ys_tm, hs_last = _s5_steps(tmaj(us), bbd, cbd, ppow, d_skip, h0s, nb, nt)
    y_sample = _s5_out(bmaj(ys_tm), zs1, xs1, w_glu, b_glu, w_so, g1, b1, ms)
    hs_re, hs_im = from_slab_state(hs_last)

    kv5 = lambda a, bsz, t: a.reshape(1, bsz, t, N_HEADS, HEAD_DIM)
    from_t = lambda a: a.reshape(N_HEADS, HEAD_DIM, l).transpose(2, 0, 1)[None, None]
    return (y_prompt.reshape(1, l, D_MODEL),
            y_sample.reshape(nb, nt, D_MODEL),
            from_t(kt), from_t(vt), lf.reshape(1, 1, l, N_HEADS),
            kv5(ks, nb, nt), kv5(vs, nb, nt), lfs.reshape(1, nb, nt, N_HEADS),
            hp_re[None], hp_im[None], hs_re[None], hs_im[None])
```

```python
import functools
import math

import jax
import jax.numpy as jnp
from jax import lax
from jax.experimental import pallas as pl
from jax.experimental.pallas import tpu as pltpu

F32 = jnp.float32
BF16 = jnp.bfloat16

D_MODEL = 1024
N_HEADS = 16
HEAD_DIM = 64
PAGE = 128
DEPTH = 2
ALPHA = (2 * DEPTH) ** 0.25
LN_EPS = 1e-5
QK_SCALE = HEAD_DIM ** -0.5
LOG2E = math.log2(math.e)
DEAD_GAP = 152.0
BOUND_SLACK = 2.0
DEAD_NAT = 106.0

LANES = 128
N_SLABS = D_MODEL // LANES
S5_GROUP = 16
S5_STATE = 64
GROUPS_PER_SLAB = LANES // S5_GROUP
SLAB_STATE = GROUPS_PER_SLAB * S5_STATE
S5_CHUNK = 64
DECODE_GROUP = 8
VMEM_LIMIT = 56 * 1024 * 1024

NT_DIMS = (((1,), (1,)), ((), ()))


def _cparams(sem):
    return pltpu.CompilerParams(dimension_semantics=sem, vmem_limit_bytes=VMEM_LIMIT)


def _split2(x):
    hi = x.astype(BF16)
    lo = (x - hi.astype(F32)).astype(BF16)
    return hi, lo


def _split3(x):
    hi = x.astype(BF16)
    r = x - hi.astype(F32)
    mid = r.astype(BF16)
    lo = (r - mid.astype(F32)).astype(BF16)
    return hi, mid, lo


def _layer_norm(r, g, b):
    mu = jnp.mean(r, axis=-1, keepdims=True)
    c = r - mu
    var = jnp.mean(c * c, axis=-1, keepdims=True)
    return c * lax.rsqrt(var + LN_EPS) * g + b


def _silu(z):
    return z * jax.nn.sigmoid(z)


def _attn_in_kernel(x_ref, w_ref, wf_ref, bf_ref, q_ref, k_ref, v_ref, kb_ref, vb_ref, z_ref, lf_ref):
    xb = x_ref[...].astype(BF16)
    q = jnp.dot(xb, w_ref[:, 0:D_MODEL], preferred_element_type=F32)
    q_ref[...] = (q * QK_SCALE).astype(BF16)
    k = jnp.dot(xb, w_ref[:, D_MODEL:2 * D_MODEL], preferred_element_type=F32)
    k_ref[...] = k
    kb_ref[...] = k.astype(BF16)
    v = jnp.dot(xb, w_ref[:, 2 * D_MODEL:3 * D_MODEL], preferred_element_type=F32)
    v_ref[...] = v
    vb_ref[...] = v.astype(BF16)
    z_ref[...] = jnp.dot(xb, w_ref[:, 3 * D_MODEL:4 * D_MODEL], preferred_element_type=F32)
    hf = jnp.dot(xb, wf_ref[...], preferred_element_type=F32) + bf_ref[...]
    lf = jnp.minimum(hf, 0.0) - jnp.log(1.0 + jnp.exp(-jnp.abs(hf)))
    lf_ref[...] = lf[:, 0:N_HEADS]


def _attn_in_proj(x, w_qkvz, w_f, b_f, tm):
    m = x.shape[0]
    row = lambda i: (i, 0)
    const = lambda i: (0, 0)
    wide = pl.BlockSpec((tm, D_MODEL), row)
    return pl.pallas_call(
        _attn_in_kernel,
        grid=(m // tm,),
        in_specs=[wide,
                  pl.BlockSpec((D_MODEL, 4 * D_MODEL), const),
                  pl.BlockSpec((D_MODEL, LANES), const),
                  pl.BlockSpec((1, LANES), const)],
        out_specs=[wide, wide, wide, wide, wide, wide, pl.BlockSpec((tm, N_HEADS), row)],
        out_shape=[jax.ShapeDtypeStruct((m, D_MODEL), BF16),
                   jax.ShapeDtypeStruct((m, D_MODEL), F32),
                   jax.ShapeDtypeStruct((m, D_MODEL), F32),
                   jax.ShapeDtypeStruct((m, D_MODEL), BF16),
                   jax.ShapeDtypeStruct((m, D_MODEL), BF16),
                   jax.ShapeDtypeStruct((m, D_MODEL), F32),
                   jax.ShapeDtypeStruct((m, N_HEADS), F32)],
        compiler_params=_cparams(("arbitrary",)),
        name="attn_in_proj",
    )(x, w_qkvz, w_f, b_f)


def _attn_in_prompt_kernel(x_ref, w_ref, wt_ref, wf_ref, bf_ref,
                           q_ref, vb_ref, z_ref, kt_ref, ktb_ref, vt_ref, lf_ref, kn_ref):
    xb = x_ref[...].astype(BF16)
    q = jnp.dot(xb, w_ref[:, 0:D_MODEL], preferred_element_type=F32)
    q_ref[...] = (q * (QK_SCALE * LOG2E)).astype(BF16)
    vb_ref[...] = jnp.dot(xb, w_ref[:, D_MODEL:2 * D_MODEL], preferred_element_type=F32).astype(BF16)
    z_ref[...] = jnp.dot(xb, w_ref[:, 2 * D_MODEL:3 * D_MODEL], preferred_element_type=F32)
    kt = lax.dot_general(wt_ref[0:D_MODEL, :], xb, NT_DIMS, preferred_element_type=F32)
    kt_ref[...] = kt
    ktb = kt.astype(BF16)
    ktb_ref[...] = ktb
    kf = ktb.astype(F32)
    norm2 = jnp.sum((kf * kf).reshape(N_HEADS, HEAD_DIM, kf.shape[1]), axis=1)
    blockmax = jnp.max(norm2, axis=1, keepdims=True)

    @pl.when(pl.program_id(0) == 0)
    def _():
        kn_ref[...] = jnp.zeros_like(kn_ref)

    kn_ref[...] = jnp.maximum(kn_ref[...], blockmax)
    vt_ref[...] = lax.dot_general(wt_ref[D_MODEL:2 * D_MODEL, :], xb, NT_DIMS, preferred_element_type=F32)
    hf = jnp.dot(xb, wf_ref[...], preferred_element_type=F32) + bf_ref[...]
    lf = jnp.minimum(hf, 0.0) - jnp.log(1.0 + jnp.exp(-jnp.abs(hf)))
    lf_ref[...] = lf[:, 0:N_HEADS]


def _attn_in_proj_prompt(x, w_qvz, wt_kv, w_f, b_f, tm):
    m = x.shape[0]
    row = lambda i: (i, 0)
    col = lambda i: (0, i)
    const = lambda i: (0, 0)
    wide = pl.BlockSpec((tm, D_MODEL), row)
    tall = pl.BlockSpec((D_MODEL, tm), col)
    return pl.pallas_call(
        _attn_in_prompt_kernel,
        grid=(m // tm,),
        in_specs=[wide,
                  pl.BlockSpec((D_MODEL, 3 * D_MODEL), const),
                  pl.BlockSpec((2 * D_MODEL, D_MODEL), const),
                  pl.BlockSpec((D_MODEL, LANES), const),
                  pl.BlockSpec((1, LANES), const)],
        out_specs=[wide, wide, wide, tall, tall, tall, pl.BlockSpec((tm, N_HEADS), row),
                   pl.BlockSpec((N_HEADS, LANES), const)],
        out_shape=[jax.ShapeDtypeStruct((m, D_MODEL), BF16),
                   jax.ShapeDtypeStruct((m, D_MODEL), BF16),
                   jax.ShapeDtypeStruct((m, D_MODEL), F32),
                   jax.ShapeDtypeStruct((D_MODEL, m), F32),
                   jax.ShapeDtypeStruct((D_MODEL, m), BF16),
                   jax.ShapeDtypeStruct((D_MODEL, m), F32),
                   jax.ShapeDtypeStruct((m, N_HEADS), F32),
                   jax.ShapeDtypeStruct((N_HEADS, LANES), F32)],
        compiler_params=_cparams(("arbitrary",)),
        name="attn_in_proj_prompt",
    )(x, w_qvz, wt_kv, w_f, b_f)


def _cumsum_kernel(lf_ref, ct_ref, carry_ref, *, tc):
    @pl.when(pl.program_id(0) == 0)
    def _():
        carry_ref[...] = jnp.zeros_like(carry_ref)

    r = lax.broadcasted_iota(jnp.int32, (tc, tc), 0)
    c = lax.broadcasted_iota(jnp.int32, (tc, tc), 1)
    tri = jnp.where(r >= c, 1.0, 0.0).astype(BF16)
    acc = carry_ref[...]
    for part in _split3(lf_ref[...]):
        acc = acc + jnp.dot(tri, part, preferred_element_type=F32)
    carry_ref[...] = acc[tc - 1:tc, :]
    hr = lax.broadcasted_iota(jnp.int32, (N_HEADS, N_HEADS), 0)
    hc = lax.broadcasted_iota(jnp.int32, (N_HEADS, N_HEADS), 1)
    eye = jnp.where(hr == hc, 1.0, 0.0).astype(BF16)
    out = jnp.zeros((N_HEADS, tc), F32)
    for part in _split3(acc):
        out = out + lax.dot_general(eye, part, NT_DIMS, preferred_element_type=F32)
    ct_ref[...] = out


def _cumsum_t(logf, tc=256):
    l = logf.shape[0]
    return pl.pallas_call(
        functools.partial(_cumsum_kernel, tc=tc),
        grid=(l // tc,),
        in_specs=[pl.BlockSpec((tc, N_HEADS), lambda i: (i, 0))],
        out_specs=pl.BlockSpec((N_HEADS, tc), lambda i: (0, i)),
        out_shape=jax.ShapeDtypeStruct((N_HEADS, l), F32),
        scratch_shapes=[pltpu.VMEM((1, N_HEADS), F32)],
        compiler_params=_cparams(("arbitrary",)),
        name="logf_cumsum",
    )(logf)


def _attn_prompt_kernel(q_ref, kt_ref, v_ref, ct_ref, kn_ref, o_ref, s0_ref, s1_ref, m_ref, acc_ref, *, tq):
    p = pl.program_id(0)
    qi = pl.program_id(1)
    q = q_ref[...]
    lane = lax.broadcasted_iota(jnp.int32, (1, LANES), 1)
    first = lane < HEAD_DIM
    zero = jnp.zeros_like(q)
    qs = (jnp.where(first, q, zero), jnp.where(first, zero, q))
    ones_col = (jnp.where(lane == HEAD_DIM, 1.0, 0.0).astype(BF16), jnp.where(lane == 0, 1.0, 0.0).astype(BF16))
    q0 = pl.multiple_of(qi * tq, tq)
    c0 = [ct_ref[pl.ds(2 * p + e, 1), pl.ds(q0, tq)][:, 0:1] for e in range(2)]
    row = lax.broadcasted_iota(jnp.int32, (tq, tq), 0)
    col = lax.broadcasted_iota(jnp.int32, (tq, tq), 1)
    causal = col <= row

    def scores(j, dst_ref):
        k0 = pl.multiple_of(j * tq, tq)
        kt = kt_ref[:, pl.ds(k0, tq)]
        for e in range(2):
            bias = (c0[e] - ct_ref[pl.ds(2 * p + e, 1), pl.ds(k0, tq)]) * LOG2E
            dst_ref[e] = jnp.dot(qs[e], kt, preferred_element_type=F32) + bias

    def absorb(j, src_ref, masked):
        k0 = pl.multiple_of(j * tq, tq)
        v = v_ref[pl.ds(k0, tq), :]
        vs = (jnp.where(first, v, ones_col[0]), jnp.where(first, ones_col[1], v))
        for e in range(2):
            s = src_ref[e]
            if masked:
                s = jnp.where(causal, s, -jnp.inf)
            m = m_ref[e]
            m_new = jnp.maximum(m, jnp.max(s, axis=1, keepdims=True))
            pr = jnp.exp2(s - m_new).astype(BF16)
            acc_ref[e] = jnp.exp2(m - m_new) * acc_ref[e] + jnp.dot(pr, vs[e], preferred_element_type=F32)
            m_ref[e] = m_new

    m_ref[...] = jnp.full(m_ref.shape, -jnp.inf, F32)
    acc_ref[...] = jnp.zeros(acc_ref.shape, F32)
    scores(qi, s0_ref)
    scores(jnp.maximum(qi - 1, 0), s1_ref)
    absorb(qi, s0_ref, True)

    pos = lax.broadcasted_iota(jnp.int32, (1, ct_ref.shape[1]), 1)
    n_blocks = 0
    for e in range(2):
        m_min = jnp.min(m_ref[e], axis=0, keepdims=True)
        qf = qs[e].astype(F32)
        qn2 = jnp.max(jnp.sum(qf * qf, axis=1, keepdims=True), axis=0, keepdims=True)
        kn2 = kn_ref[pl.ds(2 * p + e, 1), :][:, 0:1]
        reach = jnp.sqrt(qn2 * kn2) + BOUND_SLACK
        upper = (c0[e] - ct_ref[pl.ds(2 * p + e, 1), :]) * LOG2E + reach
        live = jnp.logical_and(upper >= m_min - DEAD_GAP, pos < q0)
        count = jnp.sum(jnp.where(live, 1.0, 0.0)).astype(jnp.int32)
        n_blocks = jnp.maximum(n_blocks, (count + (tq - 1)) // tq)

    def visit(t, _):
        j = qi - t
        nxt = jnp.maximum(j - 1, 0)

        @pl.when(t % 2 == 1)
        def _():
            scores(nxt, s0_ref)
            absorb(j, s1_ref, False)

        @pl.when(t % 2 == 0)
        def _():
            scores(nxt, s1_ref)
            absorb(j, s0_ref, False)

        return 0

    lax.fori_loop(1, n_blocks + 1, visit, 0)
    acca, accb = acc_ref[0], acc_ref[1]
    o_ref[...] = jnp.where(first, acca / acca[:, HEAD_DIM:HEAD_DIM + 1], accb / accb[:, 0:1])


def _attn_prompt(q, ktb, vb, ct, kn2, tq=512):
    l = q.shape[0]
    return pl.pallas_call(
        functools.partial(_attn_prompt_kernel, tq=tq),
        grid=(N_SLABS, l // tq),
        in_specs=[pl.BlockSpec((tq, LANES), lambda p, i: (i, p)),
                  pl.BlockSpec((LANES, l), lambda p, i: (p, 0)),
                  pl.BlockSpec((l, LANES), lambda p, i: (0, p)),
                  pl.BlockSpec((N_HEADS, l), lambda p, i: (0, 0)),
                  pl.BlockSpec((N_HEADS, LANES), lambda p, i: (0, 0))],
        out_specs=pl.BlockSpec((tq, LANES), lambda p, i: (i, p)),
        out_shape=jax.ShapeDtypeStruct((l, D_MODEL), F32),
        scratch_shapes=[pltpu.VMEM((2, tq, tq), F32), pltpu.VMEM((2, tq, tq), F32),
                        pltpu.VMEM((2, tq, 1), F32), pltpu.VMEM((2, tq, LANES), F32)],
        compiler_params=_cparams(("arbitrary", "arbitrary")),
        name="attn_prompt",
    )(q, ktb, vb, ct, kn2)


def _attn_out_kernel(o_ref, z_ref, x_ref, w_ref, g_ref, b_ref, y_ref):
    a = (o_ref[...] * _silu(z_ref[...])).astype(BF16)
    sub = jnp.dot(a, w_ref[...], preferred_element_type=F32)
    y_ref[...] = _layer_norm(ALPHA * x_ref[...] + sub, g_ref[...], b_ref[...])


def _attn_out(o, z, x, w_out, g, b, tm):
    m = x.shape[0]
    wide = pl.BlockSpec((tm, D_MODEL), lambda i: (i, 0))
    const = lambda i: (0, 0)
    return pl.pallas_call(
        _attn_out_kernel,
        grid=(m // tm,),
        in_specs=[wide, wide, wide, pl.BlockSpec((D_MODEL, D_MODEL), const),
                  pl.BlockSpec((1, D_MODEL), const), pl.BlockSpec((1, D_MODEL), const)],
        out_specs=wide,
        out_shape=jax.ShapeDtypeStruct((m, D_MODEL), F32),
        compiler_params=_cparams(("arbitrary",)),
        name="attn_out_ln",
    )(o, z, x, w_out, g, b)


def _s5_in_kernel(x_ref, w_ref, u_ref, z_ref):
    xb = x_ref[...].astype(BF16)
    u_ref[...] = jnp.dot(xb, w_ref[:, 0:D_MODEL], preferred_element_type=F32)
    z_ref[...] = jnp.dot(xb, w_ref[:, D_MODEL:2 * D_MODEL], preferred_element_type=F32)


def _s5_in_proj(x, w_in, tm):
    m = x.shape[0]
    wide = pl.BlockSpec((tm, D_MODEL), lambda i: (i, 0))
    return pl.pallas_call(
        _s5_in_kernel,
        grid=(m // tm,),
        in_specs=[wide, pl.BlockSpec((D_MODEL, 2 * D_MODEL), lambda i: (0, 0))],
        out_specs=[wide, wide],
        out_shape=[jax.ShapeDtypeStruct((m, D_MODEL), F32)] * 2,
        compiler_params=_cparams(("arbitrary",)),
        name="s5_in_proj",
    )(x, w_in)


def _s5_param_kernel(are_ref, aim_ref, ldt_ref, arer_ref, aimr_ref, ldtr_ref, bre_ref, bim_ref,
                     pre_ref, pim_ref, nre_ref, nim_ref, bbre_ref, bbim_ref):
    def abar(are, aim, ldt):
        dt = jnp.exp(ldt)
        mag = jnp.exp(are * dt)
        ang = aim * dt
        return mag * jnp.cos(ang), mag * jnp.sin(ang)

    are, aim = are_ref[...], aim_ref[...]
    ar, ai = abar(are, aim, ldt_ref[...])
    inv = 1.0 / (ar * ar + ai * ai)
    nr, ni = ar * inv, -ai * inv
    pr, pi = jnp.ones_like(ar), jnp.zeros_like(ar)
    qr, qi = pr, pi
    for t in range(S5_CHUNK):
        pre_ref[t] = pr
        pim_ref[t] = pi
        nre_ref[t] = qr
        nim_ref[t] = qi
        pr, pi = pr * ar - pi * ai, pr * ai + pi * ar
        qr, qi = qr * nr - qi * ni, qr * ni + qi * nr

    lr, li = arer_ref[...], aimr_ref[...]
    er, ei = abar(lr, li, ldtr_ref[...])
    linv = 1.0 / (lr * lr + li * li)
    fr = ((er - 1.0) * lr + ei * li) * linv
    fi = (ei * lr - (er - 1.0) * li) * linv
    br, bi = bre_ref[...], bim_ref[...]
    bbre_ref[...] = fr * br - fi * bi
    bbim_ref[...] = fr * bi + fi * br


def _s5_params(a_re, a_im, log_dt, b_re, b_im):
    g, p = a_re.shape
    rep = lambda a: jnp.repeat(a, S5_GROUP, axis=0)
    ldt = jnp.broadcast_to(log_dt[:, None], (g, p))
    b_re_t = b_re.transpose(0, 2, 1).reshape(g * S5_GROUP, p)
    b_im_t = b_im.transpose(0, 2, 1).reshape(g * S5_GROUP, p)
    tab = jax.ShapeDtypeStruct((S5_CHUNK, g, p), F32)
    big = jax.ShapeDtypeStruct((g * S5_GROUP, p), F32)
    return pl.pallas_call(
        _s5_param_kernel,
        out_shape=[tab, tab, tab, tab, big, big],
        name="s5_discretise",
    )(a_re, a_im, ldt, rep(a_re), rep(a_im), rep(ldt), b_re_t, b_im_t)


def _slab_table(t_re, t_im):
    t = t_re.shape[0]
    f = lambda a: a.reshape(t, N_SLABS, SLAB_STATE).transpose(1, 0, 2)
    return jnp.concatenate([f(t_re), f(t_im)], axis=-1)


def _s5_weights(bb_re, bb_im, c_re, c_im):
    eye = jnp.eye(GROUPS_PER_SLAB, dtype=F32)
    shape_b = (N_SLABS, GROUPS_PER_SLAB, S5_GROUP, S5_STATE)

    def in_mat(b):
        return jnp.einsum('kgjp,gh->kgjhp', b.reshape(shape_b), eye).reshape(N_SLABS, LANES, SLAB_STATE)

    def out_mat(c):
        return jnp.einsum('kgip,gh->kgphi', c.reshape(shape_b), eye).reshape(N_SLABS, SLAB_STATE, LANES)

    bbd = jnp.concatenate([in_mat(bb_re), in_mat(bb_im)], axis=-1).astype(BF16)
    cbd = jnp.concatenate([out_mat(c_re), -out_mat(c_im)], axis=1).astype(BF16)
    return bbd, cbd


def _cmul(ar, ai, br, bi):
    return ar * br - ai * bi, ar * bi + ai * br


def _s5_scan_kernel(u_ref, bbd_ref, cbd_ref, np_ref, pp_ref, d_ref, h0_ref, y_ref, hl_ref, hc_ref,
                    *, tr):
    t = S5_CHUNK
    half = SLAB_STATE

    @pl.when(pl.program_id(0) == 0)
    def _():
        hc_ref[...] = h0_ref[...]

    r = lax.broadcasted_iota(jnp.int32, (t, t), 0)
    c = lax.broadcasted_iota(jnp.int32, (t, t), 1)
    tri = jnp.where(r >= c, 1.0, 0.0).astype(BF16)

    for k in range(N_SLABS):
        cols = slice(k * LANES, (k + 1) * LANES)
        uk = u_ref[:, cols]
        bu = jnp.dot(uk.astype(BF16), bbd_ref[k], preferred_element_type=F32)
        nr, ni = np_ref[k, :, 0:half], np_ref[k, :, half:]
        pr, pi = pp_ref[k, :, 0:half], pp_ref[k, :, half:]
        ar, ai = pp_ref[k, 1:2, 0:half], pp_ref[k, 1:2, half:]
        hp = hc_ref[k]
        hpr, hpi = hp[:, 0:half], hp[:, half:]
        hh = []
        for ci in range(tr // t):
            rows = slice(ci * t, (ci + 1) * t)
            zr, zi = _cmul(nr, ni, bu[rows, 0:half], bu[rows, half:])
            zz = jnp.concatenate([zr, zi], axis=1).astype(BF16)
            s = jnp.dot(tri, zz, preferred_element_type=F32)
            gr, gi = _cmul(ar, ai, hpr, hpi)
            hr, hi = _cmul(pr, pi, s[:, 0:half] + gr, s[:, half:] + gi)
            hpr, hpi = hr[t - 1:t], hi[t - 1:t]
            hh.append(jnp.concatenate([hr, hi], axis=1).astype(BF16))
        hc_ref[k] = jnp.concatenate([hpr, hpi], axis=1)
        y = jnp.dot(jnp.concatenate(hh, axis=0), cbd_ref[k], preferred_element_type=F32)
        y_ref[:, cols] = y + d_ref[:, cols] * uk
    hl_ref[...] = hc_ref[...]


def _s5_scan(u, bbd, cbd, npow, ppow, d, h0, tr=512):
    l = u.shape[0]
    c3 = lambda i: (0, 0, 0)
    state = pl.BlockSpec((N_SLABS, 1, 2 * SLAB_STATE), c3)
    tabspec = pl.BlockSpec((N_SLABS, S5_CHUNK, 2 * SLAB_STATE), c3)
    return pl.pallas_call(
        functools.partial(_s5_scan_kernel, tr=tr),
        grid=(l // tr,),
        in_specs=[pl.BlockSpec((tr, D_MODEL), lambda i: (i, 0)),
                  pl.BlockSpec((N_SLABS, LANES, 2 * SLAB_STATE), c3),
                  pl.BlockSpec((N_SLABS, 2 * SLAB_STATE, LANES), c3),
                  tabspec, tabspec,
                  pl.BlockSpec((1, D_MODEL), lambda i: (0, 0)),
                  state],
        out_specs=[pl.BlockSpec((tr, D_MODEL), lambda i: (i, 0)), state],
        out_shape=[jax.ShapeDtypeStruct((l, D_MODEL), F32),
                   jax.ShapeDtypeStruct((N_SLABS, 1, 2 * SLAB_STATE), F32)],
        scratch_shapes=[pltpu.VMEM((N_SLABS, 1, 2 * SLAB_STATE), F32)],
        compiler_params=_cparams(("arbitrary",)),
        name="s5_scan_prompt",
    )(u, bbd, cbd, npow, ppow, d, h0)


def _s5_step_kernel(u_ref, bbd_ref, cbd_ref, pp_ref, d_ref, h0_ref, y_ref, hl_ref, *, nb, nt):
    half = SLAB_STATE
    for k in range(N_SLABS):
        cols = slice(k * LANES, (k + 1) * LANES)
        uk = u_ref[:, cols]
        bu = jnp.dot(uk.astype(BF16), bbd_ref[k], preferred_element_type=F32)
        ar, ai = pp_ref[k, 1:2, 0:half], pp_ref[k, 1:2, half:]
        h = h0_ref[k]
        hr, hi = h[:, 0:half], h[:, half:]
        hs = []
        for t in range(nt):
            rows = slice(t * nb, (t + 1) * nb)
            gr, gi = _cmul(ar, ai, hr, hi)
            hr, hi = gr + bu[rows, 0:half], gi + bu[rows, half:]
            hs.append(jnp.concatenate([hr, hi], axis=1))
        hl_ref[k] = hs[-1]
        hh = jnp.concatenate(hs, axis=0).astype(BF16)
        y_ref[:, cols] = jnp.dot(hh, cbd_ref[k], preferred_element_type=F32) + d_ref[:, cols] * uk


def _s5_steps(u_tm, bbd, cbd, ppow, d, h0, nb, nt):
    return pl.pallas_call(
        functools.partial(_s5_step_kernel, nb=nb, nt=nt),
        out_shape=[jax.ShapeDtypeStruct((nt * nb, D_MODEL), F32),
                   jax.ShapeDtypeStruct((N_SLABS, nb, 2 * SLAB_STATE), F32)],
        compiler_params=pltpu.CompilerParams(vmem_limit_bytes=VMEM_LIMIT),
        name="s5_scan_sample",
    )(u_tm, bbd, cbd, ppow, d, h0)


def _s5_out_kernel(y_ref, z_ref, x_ref, wg_ref, bg_ref, wo_ref, g_ref, b_ref, o_ref):
    g = jax.nn.gelu(y_ref[...])
    gate = jax.nn.sigmoid(jnp.dot(g.astype(BF16), wg_ref[...], preferred_element_type=F32) + bg_ref[...])
    a = (g * gate * _silu(z_ref[...])).astype(BF16)
    sub = jnp.dot(a, wo_ref[...], preferred_element_type=F32)
    o_ref[...] = _layer_norm(ALPHA * x_ref[...] + sub, g_ref[...], b_ref[...])


def _s5_out(y, z, x, w_glu, b_glu, w_out, g, b, tm):
    m = x.shape[0]
    wide = pl.BlockSpec((tm, D_MODEL), lambda i: (i, 0))
    const = lambda i: (0, 0)
    mat = pl.BlockSpec((D_MODEL, D_MODEL), const)
    vec = pl.BlockSpec((1, D_MODEL), const)
    return pl.pallas_call(
        _s5_out_kernel,
        grid=(m // tm,),
        in_specs=[wide, wide, wide, mat, vec, mat, vec, vec],
        out_specs=wide,
        out_shape=jax.ShapeDtypeStruct((m, D_MODEL), F32),
        compiler_params=_cparams(("arbitrary",)),
        name="s5_out_ln",
    )(y, z, x, w_glu, b_glu, w_out, g, b)


def _shift_pages(a, d):
    return jnp.concatenate([a[d:], jnp.zeros((d,) + a.shape[1:], a.dtype)], axis=0)


def _past_bias_kernel(pt_ref, lf_hbm, y_ref, buf_ref, sem, *, n_pages):
    b = pl.program_id(0)
    slot = b % 2

    def page_copy(bb, sl, i):
        return pltpu.make_async_copy(lf_hbm.at[pt_ref[bb * n_pages + i]], buf_ref.at[sl, i], sem.at[sl])

    def gather(bb, sl):
        def start(i, _):
            page_copy(bb, sl, i).start()
            return 0
        lax.fori_loop(0, n_pages, start, 0)

    @pl.when(b == 0)
    def _():
        gather(0, 0)

    @pl.when(b + 1 < pl.num_programs(0))
    def _():
        gather(b + 1, 1 - slot)

    def wait(i, _):
        page_copy(b, slot, i).wait()
        return 0

    lax.fori_loop(0, n_pages, wait, 0)

    x = buf_ref[slot].reshape(n_pages * N_HEADS, PAGE)
    kr = lax.broadcasted_iota(jnp.int32, (PAGE, 2 * PAGE), 0)
    kc = lax.broadcasted_iota(jnp.int32, (PAGE, 2 * PAGE), 1)
    later_or_all = jnp.where(jnp.logical_or(kc >= PAGE, kr > kc), 1.0, 0.0).astype(BF16)
    both = jnp.zeros((n_pages * N_HEADS, 2 * PAGE), F32)
    for part in _split3(x):
        both = both + jnp.dot(part, later_or_all, preferred_element_type=F32)
    excl = both[:, 0:PAGE].reshape(n_pages, N_HEADS, PAGE)
    after = both[:, PAGE:].reshape(n_pages, N_HEADS, PAGE)
    d = 1
    while d < n_pages:
        after = after + _shift_pages(after, d)
        d *= 2
    y_ref[0] = excl + _shift_pages(after, 1)


def _past_bias(page_table, cache_logf_l):
    nb, n_pages = page_table.shape
    lft = jnp.transpose(cache_logf_l, (0, 2, 1))
    return pl.pallas_call(
        functools.partial(_past_bias_kernel, n_pages=n_pages),
        grid_spec=pltpu.PrefetchScalarGridSpec(
            num_scalar_prefetch=1,
            grid=(nb,),
            in_specs=[pl.BlockSpec(memory_space=pl.ANY)],
            out_specs=pl.BlockSpec((1, n_pages, N_HEADS, PAGE), lambda b, pt: (b, 0, 0, 0)),
            scratch_shapes=[pltpu.VMEM((2, n_pages, N_HEADS, PAGE), F32), pltpu.SemaphoreType.DMA((2,))]),
        out_shape=jax.ShapeDtypeStruct((nb, n_pages, N_HEADS, PAGE), F32),
        compiler_params=_cparams(("arbitrary",)),
        name="past_bias",
    )(page_table.reshape(-1), lft)


def _decode_kernel(pt_ref, q_ref, kn_ref, vn_ref, lfn_ref, kc_hbm, vc_hbm, yb0_ref, ybn_ref, o_ref,
                   kbuf, vbuf, ksem, vsem, s_ref, live_ref, qbd_ref, m_ref, l_ref, acc_ref,
                   *, nb, nt, n_pages, group):
    b = pl.program_id(0)
    j = pl.program_id(1)
    n_steps = n_pages // group
    total = nb * n_steps
    step = b * n_steps + j
    cur = step % 2
    rows = nt * N_HEADS
    rsel = lax.broadcasted_iota(jnp.int32, (rows, N_HEADS), 0)
    csel = lax.broadcasted_iota(jnp.int32, (rows, N_HEADS), 1)
    sel = jnp.where((rsel & (N_HEADS - 1)) == csel, 1.0, 0.0).astype(BF16)

    def pages_of(bb, jj):
        return [pt_ref[bb * n_pages + n_pages - (jj + 1) * group + g] for g in range(group)]

    def k_copies(bb, jj, st):
        return [pltpu.make_async_copy(kc_hbm.at[pg], kbuf.at[st % 3, g], ksem.at[st % 3])
                for g, pg in enumerate(pages_of(bb, jj))]

    def v_copies(bb, jj, st):
        return [pltpu.make_async_copy(vc_hbm.at[pg], vbuf.at[st % 2, g], vsem.at[st % 2])
                for g, pg in enumerate(pages_of(bb, jj))]

    def start(copies):
        for cp in copies:
            cp.start()

    def wait(copies):
        for cp in copies:
            cp.wait()

    def following(bb, jj):
        wrap = jj + 1 == n_steps
        return jnp.where(wrap, bb + 1, bb), jnp.where(wrap, 0, jj + 1)

    b1, j1 = following(b, j)
    b2, j2 = following(b1, j1)

    @pl.when(step == 0)
    def _():
        start(k_copies(0, 0, 0) + v_copies(0, 0, 0))
        if total > 1:
            start(k_copies(b1, j1, 1))

    @pl.when(step + 2 < total)
    def _():
        start(k_copies(b2, j2, step + 2))

    def scores(st, yb_ref):
        qbd = qbd_ref[...]
        tiles = []
        for g in range(group):
            sg = jnp.dot(qbd, kbuf[st % 3, g].astype(BF16), preferred_element_type=F32)
            tiles.append(sg + jnp.concatenate([yb_ref[0, g]] * nt, axis=0))
        return jnp.concatenate(tiles, axis=1)

    @pl.when(j == 0)
    def _():
        hrow = lax.broadcasted_iota(jnp.int32, (N_HEADS, D_MODEL), 0)
        hcol = lax.broadcasted_iota(jnp.int32, (N_HEADS, D_MODEL), 1)
        own = (hcol // HEAD_DIM) == hrow
        q = q_ref[0].astype(F32)
        qbd = [jnp.where(own, jnp.broadcast_to(q[t:t + 1], (N_HEADS, D_MODEL)), 0.0) for t in range(nt)]
        qbd_ref[...] = jnp.concatenate(qbd, axis=0).astype(BF16)

        key_row = lax.broadcasted_iota(jnp.int32, (PAGE, D_MODEL), 0)
        key_row_h = lax.broadcasted_iota(jnp.int32, (PAGE, N_HEADS), 0)
        kn, vn, lfn = kn_ref[0], vn_ref[0], lfn_ref[0]
        kpad = jnp.zeros((PAGE, D_MODEL), F32)
        vpad = jnp.zeros((PAGE, D_MODEL), F32)
        ypad = jnp.zeros((PAGE, N_HEADS), F32)
        cn = jnp.zeros((1, N_HEADS), F32)
        for t in range(nt):
            cn = cn + lfn[t:t + 1]
            kpad = jnp.where(key_row == t, jnp.broadcast_to(kn[t:t + 1], (PAGE, D_MODEL)), kpad)
            vpad = jnp.where(key_row == t, jnp.broadcast_to(vn[t:t + 1], (PAGE, D_MODEL)), vpad)
            ypad = jnp.where(key_row_h == t, jnp.broadcast_to(-cn, (PAGE, N_HEADS)), ypad)
        s = lax.dot_general(qbd_ref[...], kpad.astype(BF16), NT_DIMS, preferred_element_type=F32)
        for part in _split3(ypad):
            s = s + lax.dot_general(sel, part, NT_DIMS, preferred_element_type=F32)
        srow = lax.broadcasted_iota(jnp.int32, (rows, PAGE), 0)
        scol = lax.broadcasted_iota(jnp.int32, (rows, PAGE), 1)
        s = jnp.where(scol <= (srow // N_HEADS), s, -jnp.inf)
        m = jnp.max(s, axis=1, keepdims=True)
        pr = jnp.exp(s - m)
        m_ref[...] = m
        l_ref[...] = jnp.sum(pr, axis=1, keepdims=True)
        acc_ref[...] = jnp.dot(pr.astype(BF16), vpad.astype(BF16), preferred_element_type=F32)
        wait(k_copies(b, 0, step))
        s_ref[cur] = scores(step, yb0_ref)
        live_ref[cur] = 1

    @pl.when(live_ref[cur] == 1)
    def _():
        wait(v_copies(b, j, step))
        s = s_ref[cur]
        m = m_ref[...]
        m_new = jnp.maximum(m, jnp.max(s, axis=1, keepdims=True))
        corr = jnp.exp(m - m_new)
        pr = jnp.exp(s - m_new)
        m_ref[...] = m_new
        l_ref[...] = l_ref[...] * corr + jnp.sum(pr, axis=1, keepdims=True)
        pb = pr.astype(BF16)
        pv = jnp.zeros((rows, D_MODEL), F32)
        for g in range(group):
            pv = pv + lax.dot_general(pb[:, g * PAGE:(g + 1) * PAGE], vbuf[cur, g].astype(BF16), NT_DIMS,
                                      preferred_element_type=F32)
        acc_ref[...] = acc_ref[...] * corr + pv

    @pl.when(j + 1 < n_steps)
    def _():
        wait(k_copies(b, j + 1, step + 1))
        s_next = scores(step + 1, ybn_ref)
        s_ref[1 - cur] = s_next
        gap = jnp.max(jnp.max(s_next, axis=1, keepdims=True) - m_ref[...])
        alive = gap >= -DEAD_NAT
        live_ref[1 - cur] = alive.astype(jnp.int32)

        @pl.when(alive)
        def _():
            start(v_copies(b, j + 1, step + 1))

    @pl.when(jnp.logical_and(j + 1 == n_steps, step + 1 < total))
    def _():
        start(v_copies(b + 1, 0, step + 1))

    @pl.when(j == n_steps - 1)
    def _():
        hrow = lax.broadcasted_iota(jnp.int32, (N_HEADS, D_MODEL), 0)
        hcol = lax.broadcasted_iota(jnp.int32, (N_HEADS, D_MODEL), 1)
        own = (hcol // HEAD_DIM) == hrow
        o = acc_ref[...] / l_ref[...]
        for t in range(nt):
            blk = jnp.where(own, o[t * N_HEADS:(t + 1) * N_HEADS], 0.0)
            o_ref[0, t:t + 1, :] = jnp.sum(blk, axis=0, keepdims=True)


def _decode_attn(page_table, q, k_new, v_new, lf_new, cache_k_l, cache_v_l, ybias):
    nb, n_pages = page_table.shape
    nt = q.shape[1]
    rows = nt * N_HEADS
    n_pool = cache_k_l.shape[0]
    kc = jnp.transpose(cache_k_l, (0, 2, 3, 1)).reshape(n_pool, D_MODEL, PAGE)
    vc = jnp.transpose(cache_v_l, (0, 2, 3, 1)).reshape(n_pool, D_MODEL, PAGE)

    group = math.gcd(n_pages, DECODE_GROUP)
    n_steps = n_pages // group
    per_b = lambda b, j, pt: (b, 0, 0)
    bias_spec = lambda step_of: pl.BlockSpec((1, group, N_HEADS, PAGE),
                                             lambda b, j, pt: (b, n_steps - 1 - step_of(j), 0, 0))
    return pl.pallas_call(
        functools.partial(_decode_kernel, nb=nb, nt=nt, n_pages=n_pages, group=group),
        grid_spec=pltpu.PrefetchScalarGridSpec(
            num_scalar_prefetch=1,
            grid=(nb, n_steps),
            in_specs=[pl.BlockSpec((1, nt, D_MODEL), per_b),
                      pl.BlockSpec((1, nt, D_MODEL), per_b),
                      pl.BlockSpec((1, nt, D_MODEL), per_b),
                      pl.BlockSpec((1, nt, N_HEADS), per_b),
                      pl.BlockSpec(memory_space=pl.ANY),
                      pl.BlockSpec(memory_space=pl.ANY),
                      bias_spec(lambda j: 0),
                      bias_spec(lambda j: jnp.minimum(j + 1, n_steps - 1))],
            out_specs=pl.BlockSpec((1, nt, D_MODEL), per_b),
            scratch_shapes=[pltpu.VMEM((3, group, D_MODEL, PAGE), F32),
                            pltpu.VMEM((2, group, D_MODEL, PAGE), F32),
                            pltpu.SemaphoreType.DMA((3,)),
                            pltpu.SemaphoreType.DMA((2,)),
                            pltpu.VMEM((2, rows, group * PAGE), F32),
                            pltpu.SMEM((2,), jnp.int32),
                            pltpu.VMEM((rows, D_MODEL), BF16),
                            pltpu.VMEM((rows, 1), F32),
                            pltpu.VMEM((rows, 1), F32),
                            pltpu.VMEM((rows, D_MODEL), F32)]),
        out_shape=jax.ShapeDtypeStruct((nb, nt, D_MODEL), F32),
        compiler_params=_cparams(("arbitrary", "arbitrary")),
        name="decode_attn",
    )(page_table.reshape(-1), q, k_new, v_new, lf_new, kc, vc, ybias, ybias)


def _row_tile(m):
    return 512 if m % 512 == 0 else m


def kernel(x_prompt, x_sample, cache_k, cache_v, cache_logf, page_table, state_s5_re, state_s5_im,
           attn_w_in, attn_b_f, attn_w_out, s5_w_in, s5_a_re, s5_a_im, s5_log_dt,
           s5_b_re, s5_b_im, s5_c_re, s5_c_im, s5_d, s5_w_glu, s5_b_glu, s5_w_out,
           ln_g, ln_b):
    assert x_prompt.shape[0] == 1
    l = x_prompt.shape[1]
    nb, nt = x_sample.shape[:2]
    width = N_HEADS * HEAD_DIM

    w_in = attn_w_in[0]
    w_qkvz = w_in[:, :4 * width].astype(BF16)
    w_qvz = jnp.concatenate([w_in[:, 0:width], w_in[:, 2 * width:4 * width]], axis=1).astype(BF16)
    wt_kv = w_in[:, width:3 * width].T.astype(BF16)
    w_f = jnp.pad(w_in[:, 4 * width:], ((0, 0), (0, LANES - N_HEADS))).astype(BF16)
    b_f = jnp.pad(attn_b_f[0], (0, LANES - N_HEADS)).reshape(1, LANES)
    w_ao = attn_w_out[0].astype(BF16)
    w_s5in = s5_w_in[0].astype(BF16)
    w_glu = s5_w_glu[0].astype(BF16)
    b_glu = s5_b_glu[0].reshape(1, D_MODEL)
    w_so = s5_w_out[0].astype(BF16)
    g0, b0 = ln_g[0].reshape(1, D_MODEL), ln_b[0].reshape(1, D_MODEL)
    g1, b1 = ln_g[1].reshape(1, D_MODEL), ln_b[1].reshape(1, D_MODEL)
    d_skip = s5_d[0].reshape(1, D_MODEL)

    pre, pim, nre, nim, bb_re, bb_im = _s5_params(s5_a_re[0], s5_a_im[0], s5_log_dt[0],
                                                  s5_b_re[0], s5_b_im[0])
    ppow = _slab_table(pre, pim)
    npow = _slab_table(nre, nim)
    bbd, cbd = _s5_weights(bb_re, bb_im, s5_c_re[0], s5_c_im[0])

    def to_slab_state(re, im):
        f = lambda a: a.reshape(a.shape[0], N_SLABS, SLAB_STATE).transpose(1, 0, 2)
        return jnp.concatenate([f(re), f(im)], axis=-1)

    def from_slab_state(h):
        f = lambda a: a.transpose(1, 0, 2).reshape(a.shape[1], N_SLABS * GROUPS_PER_SLAB, S5_STATE)
        return f(h[..., :SLAB_STATE]), f(h[..., SLAB_STATE:])

    xp = x_prompt[0]
    tm = _row_tile(l)
    q, vb, z, kt, ktb, vt, lf, kn2 = _attn_in_proj_prompt(xp, w_qvz, wt_kv, w_f, b_f, tm)
    ct = _cumsum_t(lf)
    o = _attn_prompt(q, ktb, vb, ct, kn2)
    x1 = _attn_out(o, z, xp, w_ao, g0, b0, tm)
    u, z1 = _s5_in_proj(x1, w_s5in, tm)
    zero_state = jnp.zeros((N_SLABS, 1, 2 * SLAB_STATE), F32)
    y, h_last = _s5_scan(u, bbd, cbd, npow, ppow, d_skip, zero_state)
    y_prompt = _s5_out(y, z1, x1, w_glu, b_glu, w_so, g1, b1, tm)
    hp_re, hp_im = from_slab_state(h_last)

    ms = nb * nt
    xs = x_sample.reshape(ms, D_MODEL)
    qs, ks, vs, _, _, zs, lfs = _attn_in_proj(xs, w_qkvz, w_f, b_f, ms)
    ybias = _past_bias(page_table, cache_logf[0])
    os_ = _decode_attn(page_table, qs.reshape(nb, nt, D_MODEL), ks.reshape(nb, nt, D_MODEL),
                       vs.reshape(nb, nt, D_MODEL), lfs.reshape(nb, nt, N_HEADS),
                       cache_k[0], cache_v[0], ybias)
    xs1 = _attn_out(os_.reshape(ms, D_MODEL), zs, xs, w_ao, g0, b0, ms)
    us, zs1 = _s5_in_proj(xs1, w_s5in, ms)
    tmaj = lambda a: a.reshape(nb, nt, D_MODEL).transpose(1, 0, 2).reshape(ms, D_MODEL)
    bmaj = lambda a: a.reshape(nt, nb, D_MODEL).transpose(1, 0, 2).reshape(ms, D_MODEL)
    h0s = to_slab_state(state_s5_re[0], state_s5_im[0])
    ys_tm, hs_last = _s5_steps(tmaj(us), bbd, cbd, ppow, d_skip, h0s, nb, nt)
    y_sample = _s5_out(bmaj(ys_tm), zs1, xs1, w_glu, b_glu, w_so, g1, b1, ms)
    hs_re, hs_im = from_slab_state(hs_last)

    kv5 = lambda a, bsz, t: a.reshape(1, bsz, t, N_HEADS, HEAD_DIM)
    from_t = lambda a: a.reshape(N_HEADS, HEAD_DIM, l).transpose(2, 0, 1)[None, None]
    return (y_prompt.reshape(1, l, D_MODEL),
            y_sample.reshape(nb, nt, D_MODEL),
            from_t(kt), from_t(vt), lf.reshape(1, 1, l, N_HEADS),
            kv5(ks, nb, nt), kv5(vs, nb, nt), lfs.reshape(1, nb, nt, N_HEADS),
            hp_re[None], hp_im[None], hs_re[None], hs_im[None])
```

```python
import functools
import math

import jax
import jax.numpy as jnp
from jax import lax
from jax.experimental import pallas as pl
from jax.experimental.pallas import tpu as pltpu

F32 = jnp.float32
BF16 = jnp.bfloat16

D_MODEL = 1024
N_HEADS = 16
HEAD_DIM = 64
PAGE = 128
DEPTH = 2
ALPHA = (2 * DEPTH) ** 0.25
LN_EPS = 1e-5
QK_SCALE = HEAD_DIM ** -0.5
LOG2E = math.log2(math.e)
DEAD_GAP = 152.0
BOUND_SLACK = 2.0
DEAD_NAT = 106.0

LANES = 128
N_SLABS = D_MODEL // LANES
S5_GROUP = 16
S5_STATE = 64
GROUPS_PER_SLAB = LANES // S5_GROUP
SLAB_STATE = GROUPS_PER_SLAB * S5_STATE
S5_CHUNK = 64
DECODE_GROUP = 8
K_AHEAD = 3
VMEM_LIMIT = 56 * 1024 * 1024

NT_DIMS = (((1,), (1,)), ((), ()))


def _cparams(sem):
    return pltpu.CompilerParams(dimension_semantics=sem, vmem_limit_bytes=VMEM_LIMIT)


def _split2(x):
    hi = x.astype(BF16)
    lo = (x - hi.astype(F32)).astype(BF16)
    return hi, lo


def _split3(x):
    hi = x.astype(BF16)
    r = x - hi.astype(F32)
    mid = r.astype(BF16)
    lo = (r - mid.astype(F32)).astype(BF16)
    return hi, mid, lo


def _layer_norm(r, g, b):
    mu = jnp.mean(r, axis=-1, keepdims=True)
    c = r - mu
    var = jnp.mean(c * c, axis=-1, keepdims=True)
    return c * lax.rsqrt(var + LN_EPS) * g + b


def _silu(z):
    return z * jax.nn.sigmoid(z)


def _attn_in_kernel(x_ref, w_ref, wf_ref, bf_ref, q_ref, k_ref, v_ref, kb_ref, vb_ref, z_ref, lf_ref):
    xb = x_ref[...].astype(BF16)
    q = jnp.dot(xb, w_ref[:, 0:D_MODEL], preferred_element_type=F32)
    q_ref[...] = (q * QK_SCALE).astype(BF16)
    k = jnp.dot(xb, w_ref[:, D_MODEL:2 * D_MODEL], preferred_element_type=F32)
    k_ref[...] = k
    kb_ref[...] = k.astype(BF16)
    v = jnp.dot(xb, w_ref[:, 2 * D_MODEL:3 * D_MODEL], preferred_element_type=F32)
    v_ref[...] = v
    vb_ref[...] = v.astype(BF16)
    z_ref[...] = jnp.dot(xb, w_ref[:, 3 * D_MODEL:4 * D_MODEL], preferred_element_type=F32)
    hf = jnp.dot(xb, wf_ref[...], preferred_element_type=F32) + bf_ref[...]
    lf = jnp.minimum(hf, 0.0) - jnp.log(1.0 + jnp.exp(-jnp.abs(hf)))
    lf_ref[...] = lf[:, 0:N_HEADS]


def _attn_in_proj(x, w_qkvz, w_f, b_f, tm):
    m = x.shape[0]
    row = lambda i: (i, 0)
    const = lambda i: (0, 0)
    wide = pl.BlockSpec((tm, D_MODEL), row)
    return pl.pallas_call(
        _attn_in_kernel,
        grid=(m // tm,),
        in_specs=[wide,
                  pl.BlockSpec((D_MODEL, 4 * D_MODEL), const),
                  pl.BlockSpec((D_MODEL, LANES), const),
                  pl.BlockSpec((1, LANES), const)],
        out_specs=[wide, wide, wide, wide, wide, wide, pl.BlockSpec((tm, N_HEADS), row)],
        out_shape=[jax.ShapeDtypeStruct((m, D_MODEL), BF16),
                   jax.ShapeDtypeStruct((m, D_MODEL), F32),
                   jax.ShapeDtypeStruct((m, D_MODEL), F32),
                   jax.ShapeDtypeStruct((m, D_MODEL), BF16),
                   jax.ShapeDtypeStruct((m, D_MODEL), BF16),
                   jax.ShapeDtypeStruct((m, D_MODEL), F32),
                   jax.ShapeDtypeStruct((m, N_HEADS), F32)],
        compiler_params=_cparams(("arbitrary",)),
        name="attn_in_proj",
    )(x, w_qkvz, w_f, b_f)


def _attn_in_prompt_kernel(x_ref, w_ref, wt_ref, wf_ref, bf_ref,
                           q_ref, vb_ref, z_ref, kt_ref, ktb_ref, vt_ref, lf_ref, kn_ref):
    xb = x_ref[...].astype(BF16)
    q = jnp.dot(xb, w_ref[:, 0:D_MODEL], preferred_element_type=F32)
    q_ref[...] = (q * (QK_SCALE * LOG2E)).astype(BF16)
    vb_ref[...] = jnp.dot(xb, w_ref[:, D_MODEL:2 * D_MODEL], preferred_element_type=F32).astype(BF16)
    z_ref[...] = jnp.dot(xb, w_ref[:, 2 * D_MODEL:3 * D_MODEL], preferred_element_type=F32)
    kt = lax.dot_general(wt_ref[0:D_MODEL, :], xb, NT_DIMS, preferred_element_type=F32)
    kt_ref[...] = kt
    ktb = kt.astype(BF16)
    ktb_ref[...] = ktb
    kf = ktb.astype(F32)
    norm2 = jnp.sum((kf * kf).reshape(N_HEADS, HEAD_DIM, kf.shape[1]), axis=1)
    blockmax = jnp.max(norm2, axis=1, keepdims=True)

    @pl.when(pl.program_id(0) == 0)
    def _():
        kn_ref[...] = jnp.zeros_like(kn_ref)

    kn_ref[...] = jnp.maximum(kn_ref[...], blockmax)
    vt_ref[...] = lax.dot_general(wt_ref[D_MODEL:2 * D_MODEL, :], xb, NT_DIMS, preferred_element_type=F32)
    hf = jnp.dot(xb, wf_ref[...], preferred_element_type=F32) + bf_ref[...]
    lf = jnp.minimum(hf, 0.0) - jnp.log(1.0 + jnp.exp(-jnp.abs(hf)))
    lf_ref[...] = lf[:, 0:N_HEADS]


def _attn_in_proj_prompt(x, w_qvz, wt_kv, w_f, b_f, tm):
    m = x.shape[0]
    row = lambda i: (i, 0)
    col = lambda i: (0, i)
    const = lambda i: (0, 0)
    wide = pl.BlockSpec((tm, D_MODEL), row)
    tall = pl.BlockSpec((D_MODEL, tm), col)
    return pl.pallas_call(
        _attn_in_prompt_kernel,
        grid=(m // tm,),
        in_specs=[wide,
                  pl.BlockSpec((D_MODEL, 3 * D_MODEL), const),
                  pl.BlockSpec((2 * D_MODEL, D_MODEL), const),
                  pl.BlockSpec((D_MODEL, LANES), const),
                  pl.BlockSpec((1, LANES), const)],
        out_specs=[wide, wide, wide, tall, tall, tall, pl.BlockSpec((tm, N_HEADS), row),
                   pl.BlockSpec((N_HEADS, LANES), const)],
        out_shape=[jax.ShapeDtypeStruct((m, D_MODEL), BF16),
                   jax.ShapeDtypeStruct((m, D_MODEL), BF16),
                   jax.ShapeDtypeStruct((m, D_MODEL), F32),
                   jax.ShapeDtypeStruct((D_MODEL, m), F32),
                   jax.ShapeDtypeStruct((D_MODEL, m), BF16),
                   jax.ShapeDtypeStruct((D_MODEL, m), F32),
                   jax.ShapeDtypeStruct((m, N_HEADS), F32),
                   jax.ShapeDtypeStruct((N_HEADS, LANES), F32)],
        compiler_params=_cparams(("arbitrary",)),
        name="attn_in_proj_prompt",
    )(x, w_qvz, wt_kv, w_f, b_f)


def _cumsum_kernel(lf_ref, ct_ref, carry_ref, *, tc):
    @pl.when(pl.program_id(0) == 0)
    def _():
        carry_ref[...] = jnp.zeros_like(carry_ref)

    r = lax.broadcasted_iota(jnp.int32, (tc, tc), 0)
    c = lax.broadcasted_iota(jnp.int32, (tc, tc), 1)
    tri = jnp.where(r >= c, 1.0, 0.0).astype(BF16)
    acc = carry_ref[...]
    for part in _split3(lf_ref[...]):
        acc = acc + jnp.dot(tri, part, preferred_element_type=F32)
    carry_ref[...] = acc[tc - 1:tc, :]
    hr = lax.broadcasted_iota(jnp.int32, (N_HEADS, N_HEADS), 0)
    hc = lax.broadcasted_iota(jnp.int32, (N_HEADS, N_HEADS), 1)
    eye = jnp.where(hr == hc, 1.0, 0.0).astype(BF16)
    out = jnp.zeros((N_HEADS, tc), F32)
    for part in _split3(acc):
        out = out + lax.dot_general(eye, part, NT_DIMS, preferred_element_type=F32)
    ct_ref[...] = out


def _cumsum_t(logf):
    l = logf.shape[0]
    tc = math.gcd(l, 1024)
    return pl.pallas_call(
        functools.partial(_cumsum_kernel, tc=tc),
        grid=(l // tc,),
        in_specs=[pl.BlockSpec((tc, N_HEADS), lambda i: (i, 0))],
        out_specs=pl.BlockSpec((N_HEADS, tc), lambda i: (0, i)),
        out_shape=jax.ShapeDtypeStruct((N_HEADS, l), F32),
        scratch_shapes=[pltpu.VMEM((1, N_HEADS), F32)],
        compiler_params=_cparams(("arbitrary",)),
        name="logf_cumsum",
    )(logf)


def _attn_prompt_kernel(q_ref, kt_ref, v_ref, ct_ref, kn_ref, o_ref, s0_ref, s1_ref, m_ref, acc_ref, *, tq):
    p = pl.program_id(0)
    qi = pl.program_id(1)
    q = q_ref[...]
    lane = lax.broadcasted_iota(jnp.int32, (1, LANES), 1)
    first = lane < HEAD_DIM
    zero = jnp.zeros_like(q)
    qs = (jnp.where(first, q, zero), jnp.where(first, zero, q))
    ones_col = (jnp.where(lane == HEAD_DIM, 1.0, 0.0).astype(BF16), jnp.where(lane == 0, 1.0, 0.0).astype(BF16))
    q0 = pl.multiple_of(qi * tq, tq)
    c0 = [ct_ref[pl.ds(2 * p + e, 1), pl.ds(q0, tq)][:, 0:1] for e in range(2)]
    row = lax.broadcasted_iota(jnp.int32, (tq, tq), 0)
    col = lax.broadcasted_iota(jnp.int32, (tq, tq), 1)
    causal = col <= row

    def scores(j, dst_ref):
        k0 = pl.multiple_of(j * tq, tq)
        kt = kt_ref[:, pl.ds(k0, tq)]
        for e in range(2):
            bias = (c0[e] - ct_ref[pl.ds(2 * p + e, 1), pl.ds(k0, tq)]) * LOG2E
            dst_ref[e] = jnp.dot(qs[e], kt, preferred_element_type=F32) + bias

    def absorb(j, src_ref, masked):
        k0 = pl.multiple_of(j * tq, tq)
        v = v_ref[pl.ds(k0, tq), :]
        vs = (jnp.where(first, v, ones_col[0]), jnp.where(first, ones_col[1], v))
        for e in range(2):
            s = src_ref[e]
            if masked:
                s = jnp.where(causal, s, -jnp.inf)
            m = m_ref[e]
            m_new = jnp.maximum(m, jnp.max(s, axis=1, keepdims=True))
            pr = jnp.exp2(s - m_new).astype(BF16)
            acc_ref[e] = jnp.exp2(m - m_new) * acc_ref[e] + jnp.dot(pr, vs[e], preferred_element_type=F32)
            m_ref[e] = m_new

    m_ref[...] = jnp.full(m_ref.shape, -jnp.inf, F32)
    acc_ref[...] = jnp.zeros(acc_ref.shape, F32)
    scores(qi, s0_ref)
    scores(jnp.maximum(qi - 1, 0), s1_ref)
    absorb(qi, s0_ref, True)

    pos = lax.broadcasted_iota(jnp.int32, (1, ct_ref.shape[1]), 1)
    n_blocks = 0
    for e in range(2):
        m_min = jnp.min(m_ref[e], axis=0, keepdims=True)
        qf = qs[e].astype(F32)
        qn2 = jnp.max(jnp.sum(qf * qf, axis=1, keepdims=True), axis=0, keepdims=True)
        kn2 = kn_ref[pl.ds(2 * p + e, 1), :][:, 0:1]
        reach = jnp.sqrt(qn2 * kn2) + BOUND_SLACK
        upper = (c0[e] - ct_ref[pl.ds(2 * p + e, 1), :]) * LOG2E + reach
        live = jnp.logical_and(upper >= m_min - DEAD_GAP, pos < q0)
        count = jnp.sum(jnp.where(live, 1.0, 0.0)).astype(jnp.int32)
        n_blocks = jnp.maximum(n_blocks, (count + (tq - 1)) // tq)

    def visit(t, _):
        j = qi - t
        nxt = jnp.maximum(j - 1, 0)

        @pl.when(t % 2 == 1)
        def _():
            scores(nxt, s0_ref)
            absorb(j, s1_ref, False)

        @pl.when(t % 2 == 0)
        def _():
            scores(nxt, s1_ref)
            absorb(j, s0_ref, False)

        return 0

    lax.fori_loop(1, n_blocks + 1, visit, 0)
    acca, accb = acc_ref[0], acc_ref[1]
    o_ref[...] = jnp.where(first, acca / acca[:, HEAD_DIM:HEAD_DIM + 1], accb / accb[:, 0:1])


def _attn_prompt(q, ktb, vb, ct, kn2, tq=512):
    l = q.shape[0]
    return pl.pallas_call(
        functools.partial(_attn_prompt_kernel, tq=tq),
        grid=(N_SLABS, l // tq),
        in_specs=[pl.BlockSpec((tq, LANES), lambda p, i: (i, p)),
                  pl.BlockSpec((LANES, l), lambda p, i: (p, 0)),
                  pl.BlockSpec((l, LANES), lambda p, i: (0, p)),
                  pl.BlockSpec((N_HEADS, l), lambda p, i: (0, 0)),
                  pl.BlockSpec((N_HEADS, LANES), lambda p, i: (0, 0))],
        out_specs=pl.BlockSpec((tq, LANES), lambda p, i: (i, p)),
        out_shape=jax.ShapeDtypeStruct((l, D_MODEL), F32),
        scratch_shapes=[pltpu.VMEM((2, tq, tq), F32), pltpu.VMEM((2, tq, tq), F32),
                        pltpu.VMEM((2, tq, 1), F32), pltpu.VMEM((2, tq, LANES), F32)],
        compiler_params=_cparams(("arbitrary", "arbitrary")),
        name="attn_prompt",
    )(q, ktb, vb, ct, kn2)


def _attn_out_kernel(o_ref, z_ref, x_ref, w_ref, g_ref, b_ref, y_ref):
    a = (o_ref[...] * _silu(z_ref[...])).astype(BF16)
    sub = jnp.dot(a, w_ref[...], preferred_element_type=F32)
    y_ref[...] = _layer_norm(ALPHA * x_ref[...] + sub, g_ref[...], b_ref[...])


def _attn_out(o, z, x, w_out, g, b, tm):
    m = x.shape[0]
    wide = pl.BlockSpec((tm, D_MODEL), lambda i: (i, 0))
    const = lambda i: (0, 0)
    return pl.pallas_call(
        _attn_out_kernel,
        grid=(m // tm,),
        in_specs=[wide, wide, wide, pl.BlockSpec((D_MODEL, D_MODEL), const),
                  pl.BlockSpec((1, D_MODEL), const), pl.BlockSpec((1, D_MODEL), const)],
        out_specs=wide,
        out_shape=jax.ShapeDtypeStruct((m, D_MODEL), F32),
        compiler_params=_cparams(("arbitrary",)),
        name="attn_out_ln",
    )(o, z, x, w_out, g, b)


def _s5_in_kernel(x_ref, w_ref, u_ref, z_ref):
    xb = x_ref[...].astype(BF16)
    u_ref[...] = jnp.dot(xb, w_ref[:, 0:D_MODEL], preferred_element_type=F32)
    z_ref[...] = jnp.dot(xb, w_ref[:, D_MODEL:2 * D_MODEL], preferred_element_type=F32)


def _s5_in_proj(x, w_in, tm):
    m = x.shape[0]
    wide = pl.BlockSpec((tm, D_MODEL), lambda i: (i, 0))
    return pl.pallas_call(
        _s5_in_kernel,
        grid=(m // tm,),
        in_specs=[wide, pl.BlockSpec((D_MODEL, 2 * D_MODEL), lambda i: (0, 0))],
        out_specs=[wide, wide],
        out_shape=[jax.ShapeDtypeStruct((m, D_MODEL), F32)] * 2,
        compiler_params=_cparams(("arbitrary",)),
        name="s5_in_proj",
    )(x, w_in)


def _s5_param_kernel(are_ref, aim_ref, ldt_ref, arer_ref, aimr_ref, ldtr_ref, bre_ref, bim_ref,
                     pre_ref, pim_ref, nre_ref, nim_ref, bbre_ref, bbim_ref):
    def abar(are, aim, ldt):
        dt = jnp.exp(ldt)
        mag = jnp.exp(are * dt)
        ang = aim * dt
        return mag * jnp.cos(ang), mag * jnp.sin(ang)

    are, aim = are_ref[...], aim_ref[...]
    ar, ai = abar(are, aim, ldt_ref[...])
    inv = 1.0 / (ar * ar + ai * ai)
    nr, ni = ar * inv, -ai * inv
    pr, pi = jnp.ones_like(ar), jnp.zeros_like(ar)
    qr, qi = pr, pi
    for t in range(S5_CHUNK):
        pre_ref[t] = pr
        pim_ref[t] = pi
        nre_ref[t] = qr
        nim_ref[t] = qi
        pr, pi = pr * ar - pi * ai, pr * ai + pi * ar
        qr, qi = qr * nr - qi * ni, qr * ni + qi * nr

    lr, li = arer_ref[...], aimr_ref[...]
    er, ei = abar(lr, li, ldtr_ref[...])
    linv = 1.0 / (lr * lr + li * li)
    fr = ((er - 1.0) * lr + ei * li) * linv
    fi = (ei * lr - (er - 1.0) * li) * linv
    br, bi = bre_ref[...], bim_ref[...]
    bbre_ref[...] = fr * br - fi * bi
    bbim_ref[...] = fr * bi + fi * br


def _s5_params(a_re, a_im, log_dt, b_re, b_im):
    g, p = a_re.shape
    rep = lambda a: jnp.repeat(a, S5_GROUP, axis=0)
    ldt = jnp.broadcast_to(log_dt[:, None], (g, p))
    b_re_t = b_re.transpose(0, 2, 1).reshape(g * S5_GROUP, p)
    b_im_t = b_im.transpose(0, 2, 1).reshape(g * S5_GROUP, p)
    tab = jax.ShapeDtypeStruct((S5_CHUNK, g, p), F32)
    big = jax.ShapeDtypeStruct((g * S5_GROUP, p), F32)
    return pl.pallas_call(
        _s5_param_kernel,
        out_shape=[tab, tab, tab, tab, big, big],
        name="s5_discretise",
    )(a_re, a_im, ldt, rep(a_re), rep(a_im), rep(ldt), b_re_t, b_im_t)


def _slab_table(t_re, t_im):
    t = t_re.shape[0]
    f = lambda a: a.reshape(t, N_SLABS, SLAB_STATE).transpose(1, 0, 2)
    return jnp.concatenate([f(t_re), f(t_im)], axis=-1)


def _s5_weights(bb_re, bb_im, c_re, c_im):
    eye = jnp.eye(GROUPS_PER_SLAB, dtype=F32)
    shape_b = (N_SLABS, GROUPS_PER_SLAB, S5_GROUP, S5_STATE)

    def in_mat(b):
        return jnp.einsum('kgjp,gh->kgjhp', b.reshape(shape_b), eye).reshape(N_SLABS, LANES, SLAB_STATE)

    def out_mat(c):
        return jnp.einsum('kgip,gh->kgphi', c.reshape(shape_b), eye).reshape(N_SLABS, SLAB_STATE, LANES)

    bbd = jnp.concatenate([in_mat(bb_re), in_mat(bb_im)], axis=-1).astype(BF16)
    cbd = jnp.concatenate([out_mat(c_re), -out_mat(c_im)], axis=1).astype(BF16)
    return bbd, cbd


def _cmul(ar, ai, br, bi):
    return ar * br - ai * bi, ar * bi + ai * br


def _s5_scan_kernel(u_ref, bbd_ref, cbd_ref, np_ref, pp_ref, d_ref, h0_ref, y_ref, hl_ref, hc_ref,
                    *, tr):
    t = S5_CHUNK
    half = SLAB_STATE

    @pl.when(pl.program_id(0) == 0)
    def _():
        hc_ref[...] = h0_ref[...]

    r = lax.broadcasted_iota(jnp.int32, (t, t), 0)
    c = lax.broadcasted_iota(jnp.int32, (t, t), 1)
    tri = jnp.where(r >= c, 1.0, 0.0).astype(BF16)

    for k in range(N_SLABS):
        cols = slice(k * LANES, (k + 1) * LANES)
        uk = u_ref[:, cols]
        bu = jnp.dot(uk.astype(BF16), bbd_ref[k], preferred_element_type=F32)
        nr, ni = np_ref[k, :, 0:half], np_ref[k, :, half:]
        pr, pi = pp_ref[k, :, 0:half], pp_ref[k, :, half:]
        ar, ai = pp_ref[k, 1:2, 0:half], pp_ref[k, 1:2, half:]
        hp = hc_ref[k]
        hpr, hpi = hp[:, 0:half], hp[:, half:]
        hh = []
        for ci in range(tr // t):
            rows = slice(ci * t, (ci + 1) * t)
            zr, zi = _cmul(nr, ni, bu[rows, 0:half], bu[rows, half:])
            zz = jnp.concatenate([zr, zi], axis=1).astype(BF16)
            s = jnp.dot(tri, zz, preferred_element_type=F32)
            gr, gi = _cmul(ar, ai, hpr, hpi)
            hr, hi = _cmul(pr, pi, s[:, 0:half] + gr, s[:, half:] + gi)
            hpr, hpi = hr[t - 1:t], hi[t - 1:t]
            hh.append(jnp.concatenate([hr, hi], axis=1).astype(BF16))
        hc_ref[k] = jnp.concatenate([hpr, hpi], axis=1)
        y = jnp.dot(jnp.concatenate(hh, axis=0), cbd_ref[k], preferred_element_type=F32)
        y_ref[:, cols] = y + d_ref[:, cols] * uk
    hl_ref[...] = hc_ref[...]


def _s5_scan(u, bbd, cbd, npow, ppow, d, h0, tr=512):
    l = u.shape[0]
    c3 = lambda i: (0, 0, 0)
    state = pl.BlockSpec((N_SLABS, 1, 2 * SLAB_STATE), c3)
    tabspec = pl.BlockSpec((N_SLABS, S5_CHUNK, 2 * SLAB_STATE), c3)
    return pl.pallas_call(
        functools.partial(_s5_scan_kernel, tr=tr),
        grid=(l // tr,),
        in_specs=[pl.BlockSpec((tr, D_MODEL), lambda i: (i, 0)),
                  pl.BlockSpec((N_SLABS, LANES, 2 * SLAB_STATE), c3),
                  pl.BlockSpec((N_SLABS, 2 * SLAB_STATE, LANES), c3),
                  tabspec, tabspec,
                  pl.BlockSpec((1, D_MODEL), lambda i: (0, 0)),
                  state],
        out_specs=[pl.BlockSpec((tr, D_MODEL), lambda i: (i, 0)), state],
        out_shape=[jax.ShapeDtypeStruct((l, D_MODEL), F32),
                   jax.ShapeDtypeStruct((N_SLABS, 1, 2 * SLAB_STATE), F32)],
        scratch_shapes=[pltpu.VMEM((N_SLABS, 1, 2 * SLAB_STATE), F32)],
        compiler_params=_cparams(("arbitrary",)),
        name="s5_scan_prompt",
    )(u, bbd, cbd, npow, ppow, d, h0)


def _s5_step_kernel(u_ref, bbd_ref, cbd_ref, pp_ref, d_ref, h0_ref, y_ref, hl_ref, *, nb, nt):
    half = SLAB_STATE
    for k in range(N_SLABS):
        cols = slice(k * LANES, (k + 1) * LANES)
        uk = u_ref[:, cols]
        bu = jnp.dot(uk.astype(BF16), bbd_ref[k], preferred_element_type=F32)
        ar, ai = pp_ref[k, 1:2, 0:half], pp_ref[k, 1:2, half:]
        h = h0_ref[k]
        hr, hi = h[:, 0:half], h[:, half:]
        hs = []
        for t in range(nt):
            rows = slice(t * nb, (t + 1) * nb)
            gr, gi = _cmul(ar, ai, hr, hi)
            hr, hi = gr + bu[rows, 0:half], gi + bu[rows, half:]
            hs.append(jnp.concatenate([hr, hi], axis=1))
        hl_ref[k] = hs[-1]
        hh = jnp.concatenate(hs, axis=0).astype(BF16)
        y_ref[:, cols] = jnp.dot(hh, cbd_ref[k], preferred_element_type=F32) + d_ref[:, cols] * uk


def _s5_steps(u_tm, bbd, cbd, ppow, d, h0, nb, nt):
    return pl.pallas_call(
        functools.partial(_s5_step_kernel, nb=nb, nt=nt),
        out_shape=[jax.ShapeDtypeStruct((nt * nb, D_MODEL), F32),
                   jax.ShapeDtypeStruct((N_SLABS, nb, 2 * SLAB_STATE), F32)],
        compiler_params=pltpu.CompilerParams(vmem_limit_bytes=VMEM_LIMIT),
        name="s5_scan_sample",
    )(u_tm, bbd, cbd, ppow, d, h0)


def _s5_out_kernel(y_ref, z_ref, x_ref, wg_ref, bg_ref, wo_ref, g_ref, b_ref, o_ref):
    g = jax.nn.gelu(y_ref[...])
    gate = jax.nn.sigmoid(jnp.dot(g.astype(BF16), wg_ref[...], preferred_element_type=F32) + bg_ref[...])
    a = (g * gate * _silu(z_ref[...])).astype(BF16)
    sub = jnp.dot(a, wo_ref[...], preferred_element_type=F32)
    o_ref[...] = _layer_norm(ALPHA * x_ref[...] + sub, g_ref[...], b_ref[...])


def _s5_out(y, z, x, w_glu, b_glu, w_out, g, b, tm):
    m = x.shape[0]
    wide = pl.BlockSpec((tm, D_MODEL), lambda i: (i, 0))
    const = lambda i: (0, 0)
    mat = pl.BlockSpec((D_MODEL, D_MODEL), const)
    vec = pl.BlockSpec((1, D_MODEL), const)
    return pl.pallas_call(
        _s5_out_kernel,
        grid=(m // tm,),
        in_specs=[wide, wide, wide, mat, vec, mat, vec, vec],
        out_specs=wide,
        out_shape=jax.ShapeDtypeStruct((m, D_MODEL), F32),
        compiler_params=_cparams(("arbitrary",)),
        name="s5_out_ln",
    )(y, z, x, w_glu, b_glu, w_out, g, b)


def _shift_pages(a, d):
    return jnp.concatenate([a[d:], jnp.zeros((d,) + a.shape[1:], a.dtype)], axis=0)


def _past_bias_kernel(pt_ref, lf_hbm, y_ref, buf_ref, sem, *, n_pages):
    b = pl.program_id(0)
    slot = b % 2

    def page_copy(bb, sl, i):
        return pltpu.make_async_copy(lf_hbm.at[pt_ref[bb * n_pages + i]], buf_ref.at[sl, i], sem.at[sl])

    def gather(bb, sl):
        def start(i, _):
            page_copy(bb, sl, i).start()
            return 0
        lax.fori_loop(0, n_pages, start, 0)

    @pl.when(b == 0)
    def _():
        gather(0, 0)

    @pl.when(b + 1 < pl.num_programs(0))
    def _():
        gather(b + 1, 1 - slot)

    def wait(i, _):
        page_copy(b, slot, i).wait()
        return 0

    lax.fori_loop(0, n_pages, wait, 0)

    x = buf_ref[slot].reshape(n_pages * N_HEADS, PAGE)
    kr = lax.broadcasted_iota(jnp.int32, (PAGE, 2 * PAGE), 0)
    kc = lax.broadcasted_iota(jnp.int32, (PAGE, 2 * PAGE), 1)
    later_or_all = jnp.where(jnp.logical_or(kc >= PAGE, kr > kc), 1.0, 0.0).astype(BF16)
    both = jnp.zeros((n_pages * N_HEADS, 2 * PAGE), F32)
    for part in _split3(x):
        both = both + jnp.dot(part, later_or_all, preferred_element_type=F32)
    excl = both[:, 0:PAGE].reshape(n_pages, N_HEADS, PAGE)
    after = both[:, PAGE:].reshape(n_pages, N_HEADS, PAGE)
    d = 1
    while d < n_pages:
        after = after + _shift_pages(after, d)
        d *= 2
    y_ref[0] = excl + _shift_pages(after, 1)


def _past_bias(page_table, cache_logf_l):
    nb, n_pages = page_table.shape
    lft = jnp.transpose(cache_logf_l, (0, 2, 1))
    return pl.pallas_call(
        functools.partial(_past_bias_kernel, n_pages=n_pages),
        grid_spec=pltpu.PrefetchScalarGridSpec(
            num_scalar_prefetch=1,
            grid=(nb,),
            in_specs=[pl.BlockSpec(memory_space=pl.ANY)],
            out_specs=pl.BlockSpec((1, n_pages, N_HEADS, PAGE), lambda b, pt: (b, 0, 0, 0)),
            scratch_shapes=[pltpu.VMEM((2, n_pages, N_HEADS, PAGE), F32), pltpu.SemaphoreType.DMA((2,))]),
        out_shape=jax.ShapeDtypeStruct((nb, n_pages, N_HEADS, PAGE), F32),
        compiler_params=_cparams(("arbitrary",)),
        name="past_bias",
    )(page_table.reshape(-1), lft)


def _decode_kernel(pt_ref, q_ref, kn_ref, vn_ref, lfn_ref, kc_hbm, vc_hbm, yb0_ref, ybn_ref, o_ref,
                   kbuf, vbuf, ksem, vsem, s_ref, live_ref, qbd_ref, m_ref, l_ref, acc_ref,
                   *, nb, nt, n_pages, group):
    b = pl.program_id(0)
    j = pl.program_id(1)
    n_steps = n_pages // group
    total = nb * n_steps
    step = b * n_steps + j
    cur = step % 2
    rows = nt * N_HEADS
    rsel = lax.broadcasted_iota(jnp.int32, (rows, N_HEADS), 0)
    csel = lax.broadcasted_iota(jnp.int32, (rows, N_HEADS), 1)
    sel = jnp.where((rsel & (N_HEADS - 1)) == csel, 1.0, 0.0).astype(BF16)

    k_slots = K_AHEAD + 1

    def pages_of(bb, jj):
        return [pt_ref[bb * n_pages + n_pages - (jj + 1) * group + g] for g in range(group)]

    def k_copies(bb, jj, st):
        return [pltpu.make_async_copy(kc_hbm.at[pg], kbuf.at[st % k_slots, g], ksem.at[st % k_slots])
                for g, pg in enumerate(pages_of(bb, jj))]

    def v_copies(bb, jj, st):
        return [pltpu.make_async_copy(vc_hbm.at[pg], vbuf.at[st % 2, g], vsem.at[st % 2])
                for g, pg in enumerate(pages_of(bb, jj))]

    def start(copies):
        for cp in copies:
            cp.start()

    def wait(copies):
        for cp in copies:
            cp.wait()

    def following(bb, jj):
        wrap = jj + 1 == n_steps
        return jnp.where(wrap, bb + 1, bb), jnp.where(wrap, 0, jj + 1)

    ahead = (b, j)
    for _ in range(K_AHEAD):
        ahead = following(*ahead)

    @pl.when(step == 0)
    def _():
        start(v_copies(0, 0, 0))
        for st in range(min(K_AHEAD, total)):
            start(k_copies(st // n_steps, st % n_steps, st))

    @pl.when(step + K_AHEAD < total)
    def _():
        start(k_copies(ahead[0], ahead[1], step + K_AHEAD))

    def scores(st, yb_ref):
        qbd = qbd_ref[...]
        tiles = []
        for g in range(group):
            sg = jnp.dot(qbd, kbuf[st % k_slots, g].astype(BF16), preferred_element_type=F32)
            tiles.append(sg + jnp.concatenate([yb_ref[0, g]] * nt, axis=0))
        return jnp.concatenate(tiles, axis=1)

    @pl.when(j == 0)
    def _():
        hrow = lax.broadcasted_iota(jnp.int32, (N_HEADS, D_MODEL), 0)
        hcol = lax.broadcasted_iota(jnp.int32, (N_HEADS, D_MODEL), 1)
        own = (hcol // HEAD_DIM) == hrow
        q = q_ref[0].astype(F32)
        qbd = [jnp.where(own, jnp.broadcast_to(q[t:t + 1], (N_HEADS, D_MODEL)), 0.0) for t in range(nt)]
        qbd_ref[...] = jnp.concatenate(qbd, axis=0).astype(BF16)

        key_row = lax.broadcasted_iota(jnp.int32, (PAGE, D_MODEL), 0)
        key_row_h = lax.broadcasted_iota(jnp.int32, (PAGE, N_HEADS), 0)
        kn, vn, lfn = kn_ref[0], vn_ref[0], lfn_ref[0]
        kpad = jnp.zeros((PAGE, D_MODEL), F32)
        vpad = jnp.zeros((PAGE, D_MODEL), F32)
        ypad = jnp.zeros((PAGE, N_HEADS), F32)
        cn = jnp.zeros((1, N_HEADS), F32)
        for t in range(nt):
            cn = cn + lfn[t:t + 1]
            kpad = jnp.where(key_row == t, jnp.broadcast_to(kn[t:t + 1], (PAGE, D_MODEL)), kpad)
            vpad = jnp.where(key_row == t, jnp.broadcast_to(vn[t:t + 1], (PAGE, D_MODEL)), vpad)
            ypad = jnp.where(key_row_h == t, jnp.broadcast_to(-cn, (PAGE, N_HEADS)), ypad)
        s = lax.dot_general(qbd_ref[...], kpad.astype(BF16), NT_DIMS, preferred_element_type=F32)
        for part in _split3(ypad):
            s = s + lax.dot_general(sel, part, NT_DIMS, preferred_element_type=F32)
        srow = lax.broadcasted_iota(jnp.int32, (rows, PAGE), 0)
        scol = lax.broadcasted_iota(jnp.int32, (rows, PAGE), 1)
        s = jnp.where(scol <= (srow // N_HEADS), s, -jnp.inf)
        m = jnp.max(s, axis=1, keepdims=True)
        pr = jnp.exp(s - m)
        m_ref[...] = m
        l_ref[...] = jnp.sum(pr, axis=1, keepdims=True)
        acc_ref[...] = jnp.dot(pr.astype(BF16), vpad.astype(BF16), preferred_element_type=F32)
        wait(k_copies(b, 0, step))
        s_ref[cur] = scores(step, yb0_ref)
        live_ref[cur] = 1

    @pl.when(live_ref[cur] == 1)
    def _():
        wait(v_copies(b, j, step))
        s = s_ref[cur]
        m = m_ref[...]
        m_new = jnp.maximum(m, jnp.max(s, axis=1, keepdims=True))
        corr = jnp.exp(m - m_new)
        pr = jnp.exp(s - m_new)
        m_ref[...] = m_new
        l_ref[...] = l_ref[...] * corr + jnp.sum(pr, axis=1, keepdims=True)
        pb = pr.astype(BF16)
        pv = jnp.zeros((rows, D_MODEL), F32)
        for g in range(group):
            pv = pv + lax.dot_general(pb[:, g * PAGE:(g + 1) * PAGE], vbuf[cur, g].astype(BF16), NT_DIMS,
                                      preferred_element_type=F32)
        acc_ref[...] = acc_ref[...] * corr + pv

    @pl.when(j + 1 < n_steps)
    def _():
        wait(k_copies(b, j + 1, step + 1))
        s_next = scores(step + 1, ybn_ref)
        s_ref[1 - cur] = s_next
        gap = jnp.max(jnp.max(s_next, axis=1, keepdims=True) - m_ref[...])
        alive = gap >= -DEAD_NAT
        live_ref[1 - cur] = alive.astype(jnp.int32)

        @pl.when(alive)
        def _():
            start(v_copies(b, j + 1, step + 1))

    @pl.when(jnp.logical_and(j + 1 == n_steps, step + 1 < total))
    def _():
        start(v_copies(b + 1, 0, step + 1))

    @pl.when(j == n_steps - 1)
    def _():
        hrow = lax.broadcasted_iota(jnp.int32, (N_HEADS, D_MODEL), 0)
        hcol = lax.broadcasted_iota(jnp.int32, (N_HEADS, D_MODEL), 1)
        own = (hcol // HEAD_DIM) == hrow
        o = acc_ref[...] / l_ref[...]
        for t in range(nt):
            blk = jnp.where(own, o[t * N_HEADS:(t + 1) * N_HEADS], 0.0)
            o_ref[0, t:t + 1, :] = jnp.sum(blk, axis=0, keepdims=True)


def _decode_attn(page_table, q, k_new, v_new, lf_new, cache_k_l, cache_v_l, ybias):
    nb, n_pages = page_table.shape
    nt = q.shape[1]
    rows = nt * N_HEADS
    n_pool = cache_k_l.shape[0]
    kc = jnp.transpose(cache_k_l, (0, 2, 3, 1)).reshape(n_pool, D_MODEL, PAGE)
    vc = jnp.transpose(cache_v_l, (0, 2, 3, 1)).reshape(n_pool, D_MODEL, PAGE)

    group = math.gcd(n_pages, DECODE_GROUP)
    n_steps = n_pages // group
    per_b = lambda b, j, pt: (b, 0, 0)
    bias_spec = lambda step_of: pl.BlockSpec((1, group, N_HEADS, PAGE),
                                             lambda b, j, pt: (b, n_steps - 1 - step_of(j), 0, 0))
    return pl.pallas_call(
        functools.partial(_decode_kernel, nb=nb, nt=nt, n_pages=n_pages, group=group),
        grid_spec=pltpu.PrefetchScalarGridSpec(
            num_scalar_prefetch=1,
            grid=(nb, n_steps),
            in_specs=[pl.BlockSpec((1, nt, D_MODEL), per_b),
                      pl.BlockSpec((1, nt, D_MODEL), per_b),
                      pl.BlockSpec((1, nt, D_MODEL), per_b),
                      pl.BlockSpec((1, nt, N_HEADS), per_b),
                      pl.BlockSpec(memory_space=pl.ANY),
                      pl.BlockSpec(memory_space=pl.ANY),
                      bias_spec(lambda j: 0),
                      bias_spec(lambda j: jnp.minimum(j + 1, n_steps - 1))],
            out_specs=pl.BlockSpec((1, nt, D_MODEL), per_b),
            scratch_shapes=[pltpu.VMEM((K_AHEAD + 1, group, D_MODEL, PAGE), F32),
                            pltpu.VMEM((2, group, D_MODEL, PAGE), F32),
                            pltpu.SemaphoreType.DMA((K_AHEAD + 1,)),
                            pltpu.SemaphoreType.DMA((2,)),
                            pltpu.VMEM((2, rows, group * PAGE), F32),
                            pltpu.SMEM((2,), jnp.int32),
                            pltpu.VMEM((rows, D_MODEL), BF16),
                            pltpu.VMEM((rows, 1), F32),
                            pltpu.VMEM((rows, 1), F32),
                            pltpu.VMEM((rows, D_MODEL), F32)]),
        out_shape=jax.ShapeDtypeStruct((nb, nt, D_MODEL), F32),
        compiler_params=_cparams(("arbitrary", "arbitrary")),
        name="decode_attn",
    )(page_table.reshape(-1), q, k_new, v_new, lf_new, kc, vc, ybias, ybias)


def _row_tile(m):
    return 512 if m % 512 == 0 else m


def kernel(x_prompt, x_sample, cache_k, cache_v, cache_logf, page_table, state_s5_re, state_s5_im,
           attn_w_in, attn_b_f, attn_w_out, s5_w_in, s5_a_re, s5_a_im, s5_log_dt,
           s5_b_re, s5_b_im, s5_c_re, s5_c_im, s5_d, s5_w_glu, s5_b_glu, s5_w_out,
           ln_g, ln_b):
    assert x_prompt.shape[0] == 1
    l = x_prompt.shape[1]
    nb, nt = x_sample.shape[:2]
    width = N_HEADS * HEAD_DIM

    w_in = attn_w_in[0]
    w_qkvz = w_in[:, :4 * width].astype(BF16)
    w_qvz = jnp.concatenate([w_in[:, 0:width], w_in[:, 2 * width:4 * width]], axis=1).astype(BF16)
    wt_kv = w_in[:, width:3 * width].T.astype(BF16)
    w_f = jnp.pad(w_in[:, 4 * width:], ((0, 0), (0, LANES - N_HEADS))).astype(BF16)
    b_f = jnp.pad(attn_b_f[0], (0, LANES - N_HEADS)).reshape(1, LANES)
    w_ao = attn_w_out[0].astype(BF16)
    w_s5in = s5_w_in[0].astype(BF16)
    w_glu = s5_w_glu[0].astype(BF16)
    b_glu = s5_b_glu[0].reshape(1, D_MODEL)
    w_so = s5_w_out[0].astype(BF16)
    g0, b0 = ln_g[0].reshape(1, D_MODEL), ln_b[0].reshape(1, D_MODEL)
    g1, b1 = ln_g[1].reshape(1, D_MODEL), ln_b[1].reshape(1, D_MODEL)
    d_skip = s5_d[0].reshape(1, D_MODEL)

    pre, pim, nre, nim, bb_re, bb_im = _s5_params(s5_a_re[0], s5_a_im[0], s5_log_dt[0],
                                                  s5_b_re[0], s5_b_im[0])
    ppow = _slab_table(pre, pim)
    npow = _slab_table(nre, nim)
    bbd, cbd = _s5_weights(bb_re, bb_im, s5_c_re[0], s5_c_im[0])

    def to_slab_state(re, im):
        f = lambda a: a.reshape(a.shape[0], N_SLABS, SLAB_STATE).transpose(1, 0, 2)
        return jnp.concatenate([f(re), f(im)], axis=-1)

    def from_slab_state(h):
        f = lambda a: a.transpose(1, 0, 2).reshape(a.shape[1], N_SLABS * GROUPS_PER_SLAB, S5_STATE)
        return f(h[..., :SLAB_STATE]), f(h[..., SLAB_STATE:])

    xp = x_prompt[0]
    tm = _row_tile(l)
    q, vb, z, kt, ktb, vt, lf, kn2 = _attn_in_proj_prompt(xp, w_qvz, wt_kv, w_f, b_f, tm)
    ct = _cumsum_t(lf)
    o = _attn_prompt(q, ktb, vb, ct, kn2)
    x1 = _attn_out(o, z, xp, w_ao, g0, b0, tm)
    u, z1 = _s5_in_proj(x1, w_s5in, tm)
    zero_state = jnp.zeros((N_SLABS, 1, 2 * SLAB_STATE), F32)
    y, h_last = _s5_scan(u, bbd, cbd, npow, ppow, d_skip, zero_state)
    y_prompt = _s5_out(y, z1, x1, w_glu, b_glu, w_so, g1, b1, tm)
    hp_re, hp_im = from_slab_state(h_last)

    ms = nb * nt
    xs = x_sample.reshape(ms, D_MODEL)
    qs, ks, vs, _, _, zs, lfs = _attn_in_proj(xs, w_qkvz, w_f, b_f, ms)
    ybias = _past_bias(page_table, cache_logf[0])
    os_ = _decode_attn(page_table, qs.reshape(nb, nt, D_MODEL), ks.reshape(nb, nt, D_MODEL),
                       vs.reshape(nb, nt, D_MODEL), lfs.reshape(nb, nt, N_HEADS),
                       cache_k[0], cache_v[0], ybias)
    xs1 = _attn_out(os_.reshape(ms, D_MODEL), zs, xs, w_ao, g0, b0, ms)
    us, zs1 = _s5_in_proj(xs1, w_s5in, ms)
    tmaj = lambda a: a.reshape(nb, nt, D_MODEL).transpose(1, 0, 2).reshape(ms, D_MODEL)
    bmaj = lambda a: a.reshape(nt, nb, D_MODEL).transpose(1, 0, 2).reshape(ms, D_MODEL)
    h0s = to_slab_state(state_s5_re[0], state_s5_im[0])
    ys_tm, hs_last = _s5_steps(tmaj(us), bbd, cbd, ppow, d_skip, h0s, nb, nt)
    y_sample = _s5_out(bmaj(ys_tm), zs1, xs1, w_glu, b_glu, w_so, g1, b1, ms)
    hs_re, hs_im = from_slab_state(hs_last)

    kv5 = lambda a, bsz, t: a.reshape(1, bsz, t, N_HEADS, HEAD_DIM)
    from_t = lambda a: a.reshape(N_HEADS, HEAD_DIM, l).transpose(2, 0, 1)[None, None]
    return (y_prompt.reshape(1, l, D_MODEL),
            y_sample.reshape(nb, nt, D_MODEL),
            from_t(kt), from_t(vt), lf.reshape(1, 1, l, N_HEADS),
            kv5(ks, nb, nt), kv5(vs, nb, nt), lfs.reshape(1, nb, nt, N_HEADS),
            hp_re[None], hp_im[None], hs_re[None], hs_im[None])
```

```python
import functools
import math

import jax
import jax.numpy as jnp
from jax import lax
from jax.experimental import pallas as pl
from jax.experimental.pallas import tpu as pltpu

F32 = jnp.float32
BF16 = jnp.bfloat16

D_MODEL = 1024
N_HEADS = 16
HEAD_DIM = 64
PAGE = 128
DEPTH = 2
ALPHA = (2 * DEPTH) ** 0.25
LN_EPS = 1e-5
QK_SCALE = HEAD_DIM ** -0.5
LOG2E = math.log2(math.e)
DEAD_GAP = 152.0
BOUND_SLACK = 2.0
DEAD_NAT = 106.0

LANES = 128
N_SLABS = D_MODEL // LANES
S5_GROUP = 16
S5_STATE = 64
GROUPS_PER_SLAB = LANES // S5_GROUP
SLAB_STATE = GROUPS_PER_SLAB * S5_STATE
S5_CHUNK = 64
DECODE_GROUP = 8
K_AHEAD = 4
VMEM_LIMIT = 56 * 1024 * 1024

NT_DIMS = (((1,), (1,)), ((), ()))


def _cparams(sem):
    return pltpu.CompilerParams(dimension_semantics=sem, vmem_limit_bytes=VMEM_LIMIT)


def _split2(x):
    hi = x.astype(BF16)
    lo = (x - hi.astype(F32)).astype(BF16)
    return hi, lo


def _split3(x):
    hi = x.astype(BF16)
    r = x - hi.astype(F32)
    mid = r.astype(BF16)
    lo = (r - mid.astype(F32)).astype(BF16)
    return hi, mid, lo


def _layer_norm(r, g, b):
    mu = jnp.mean(r, axis=-1, keepdims=True)
    c = r - mu
    var = jnp.mean(c * c, axis=-1, keepdims=True)
    return c * lax.rsqrt(var + LN_EPS) * g + b


def _silu(z):
    return z * jax.nn.sigmoid(z)


def _attn_in_kernel(x_ref, w_ref, wf_ref, bf_ref, q_ref, k_ref, v_ref, kb_ref, vb_ref, z_ref, lf_ref):
    xb = x_ref[...].astype(BF16)
    q = jnp.dot(xb, w_ref[:, 0:D_MODEL], preferred_element_type=F32)
    q_ref[...] = (q * QK_SCALE).astype(BF16)
    k = jnp.dot(xb, w_ref[:, D_MODEL:2 * D_MODEL], preferred_element_type=F32)
    k_ref[...] = k
    kb_ref[...] = k.astype(BF16)
    v = jnp.dot(xb, w_ref[:, 2 * D_MODEL:3 * D_MODEL], preferred_element_type=F32)
    v_ref[...] = v
    vb_ref[...] = v.astype(BF16)
    z_ref[...] = jnp.dot(xb, w_ref[:, 3 * D_MODEL:4 * D_MODEL], preferred_element_type=F32)
    hf = jnp.dot(xb, wf_ref[...], preferred_element_type=F32) + bf_ref[...]
    lf = jnp.minimum(hf, 0.0) - jnp.log(1.0 + jnp.exp(-jnp.abs(hf)))
    lf_ref[...] = lf[:, 0:N_HEADS]


def _attn_in_proj(x, w_qkvz, w_f, b_f, tm):
    m = x.shape[0]
    row = lambda i: (i, 0)
    const = lambda i: (0, 0)
    wide = pl.BlockSpec((tm, D_MODEL), row)
    return pl.pallas_call(
        _attn_in_kernel,
        grid=(m // tm,),
        in_specs=[wide,
                  pl.BlockSpec((D_MODEL, 4 * D_MODEL), const),
                  pl.BlockSpec((D_MODEL, LANES), const),
                  pl.BlockSpec((1, LANES), const)],
        out_specs=[wide, wide, wide, wide, wide, wide, pl.BlockSpec((tm, N_HEADS), row)],
        out_shape=[jax.ShapeDtypeStruct((m, D_MODEL), BF16),
                   jax.ShapeDtypeStruct((m, D_MODEL), F32),
                   jax.ShapeDtypeStruct((m, D_MODEL), F32),
                   jax.ShapeDtypeStruct((m, D_MODEL), BF16),
                   jax.ShapeDtypeStruct((m, D_MODEL), BF16),
                   jax.ShapeDtypeStruct((m, D_MODEL), F32),
                   jax.ShapeDtypeStruct((m, N_HEADS), F32)],
        compiler_params=_cparams(("arbitrary",)),
        name="attn_in_proj",
    )(x, w_qkvz, w_f, b_f)


def _attn_in_prompt_kernel(x_ref, w_ref, wt_ref, wf_ref, bf_ref,
                           q_ref, vb_ref, z_ref, kt_ref, ktb_ref, vt_ref, lf_ref, kn_ref):
    xb = x_ref[...].astype(BF16)
    q = jnp.dot(xb, w_ref[:, 0:D_MODEL], preferred_element_type=F32)
    q_ref[...] = (q * (QK_SCALE * LOG2E)).astype(BF16)
    vb_ref[...] = jnp.dot(xb, w_ref[:, D_MODEL:2 * D_MODEL], preferred_element_type=F32).astype(BF16)
    z_ref[...] = jnp.dot(xb, w_ref[:, 2 * D_MODEL:3 * D_MODEL], preferred_element_type=F32)
    kt = lax.dot_general(wt_ref[0:D_MODEL, :], xb, NT_DIMS, preferred_element_type=F32)
    kt_ref[...] = kt
    ktb = kt.astype(BF16)
    ktb_ref[...] = ktb
    kf = ktb.astype(F32)
    norm2 = jnp.sum((kf * kf).reshape(N_HEADS, HEAD_DIM, kf.shape[1]), axis=1)
    blockmax = jnp.max(norm2, axis=1, keepdims=True)

    @pl.when(pl.program_id(0) == 0)
    def _():
        kn_ref[...] = jnp.zeros_like(kn_ref)

    kn_ref[...] = jnp.maximum(kn_ref[...], blockmax)
    vt_ref[...] = lax.dot_general(wt_ref[D_MODEL:2 * D_MODEL, :], xb, NT_DIMS, preferred_element_type=F32)
    hf = jnp.dot(xb, wf_ref[...], preferred_element_type=F32) + bf_ref[...]
    lf = jnp.minimum(hf, 0.0) - jnp.log(1.0 + jnp.exp(-jnp.abs(hf)))
    lf_ref[...] = lf[:, 0:N_HEADS]


def _attn_in_proj_prompt(x, w_qvz, wt_kv, w_f, b_f, tm):
    m = x.shape[0]
    row = lambda i: (i, 0)
    col = lambda i: (0, i)
    const = lambda i: (0, 0)
    wide = pl.BlockSpec((tm, D_MODEL), row)
    tall = pl.BlockSpec((D_MODEL, tm), col)
    return pl.pallas_call(
        _attn_in_prompt_kernel,
        grid=(m // tm,),
        in_specs=[wide,
                  pl.BlockSpec((D_MODEL, 3 * D_MODEL), const),
                  pl.BlockSpec((2 * D_MODEL, D_MODEL), const),
                  pl.BlockSpec((D_MODEL, LANES), const),
                  pl.BlockSpec((1, LANES), const)],
        out_specs=[wide, wide, wide, tall, tall, tall, pl.BlockSpec((tm, N_HEADS), row),
                   pl.BlockSpec((N_HEADS, LANES), const)],
        out_shape=[jax.ShapeDtypeStruct((m, D_MODEL), BF16),
                   jax.ShapeDtypeStruct((m, D_MODEL), BF16),
                   jax.ShapeDtypeStruct((m, D_MODEL), F32),
                   jax.ShapeDtypeStruct((D_MODEL, m), F32),
                   jax.ShapeDtypeStruct((D_MODEL, m), BF16),
                   jax.ShapeDtypeStruct((D_MODEL, m), F32),
                   jax.ShapeDtypeStruct((m, N_HEADS), F32),
                   jax.ShapeDtypeStruct((N_HEADS, LANES), F32)],
        compiler_params=_cparams(("arbitrary",)),
        name="attn_in_proj_prompt",
    )(x, w_qvz, wt_kv, w_f, b_f)


def _cumsum_kernel(lf_ref, ct_ref, carry_ref, *, tc):
    @pl.when(pl.program_id(0) == 0)
    def _():
        carry_ref[...] = jnp.zeros_like(carry_ref)

    r = lax.broadcasted_iota(jnp.int32, (tc, tc), 0)
    c = lax.broadcasted_iota(jnp.int32, (tc, tc), 1)
    tri = jnp.where(r >= c, 1.0, 0.0).astype(BF16)
    acc = carry_ref[...]
    for part in _split3(lf_ref[...]):
        acc = acc + jnp.dot(tri, part, preferred_element_type=F32)
    carry_ref[...] = acc[tc - 1:tc, :]
    hr = lax.broadcasted_iota(jnp.int32, (N_HEADS, N_HEADS), 0)
    hc = lax.broadcasted_iota(jnp.int32, (N_HEADS, N_HEADS), 1)
    eye = jnp.where(hr == hc, 1.0, 0.0).astype(BF16)
    out = jnp.zeros((N_HEADS, tc), F32)
    for part in _split3(acc):
        out = out + lax.dot_general(eye, part, NT_DIMS, preferred_element_type=F32)
    ct_ref[...] = out


def _cumsum_t(logf):
    l = logf.shape[0]
    tc = math.gcd(l, 1024)
    return pl.pallas_call(
        functools.partial(_cumsum_kernel, tc=tc),
        grid=(l // tc,),
        in_specs=[pl.BlockSpec((tc, N_HEADS), lambda i: (i, 0))],
        out_specs=pl.BlockSpec((N_HEADS, tc), lambda i: (0, i)),
        out_shape=jax.ShapeDtypeStruct((N_HEADS, l), F32),
        scratch_shapes=[pltpu.VMEM((1, N_HEADS), F32)],
        compiler_params=_cparams(("arbitrary",)),
        name="logf_cumsum",
    )(logf)


def _attn_prompt_kernel(q_ref, kt_ref, v_ref, ct_ref, kn_ref, o_ref, s0_ref, s1_ref, m_ref, acc_ref, *, tq):
    p = pl.program_id(0)
    qi = pl.program_id(1)
    q = q_ref[...]
    lane = lax.broadcasted_iota(jnp.int32, (1, LANES), 1)
    first = lane < HEAD_DIM
    zero = jnp.zeros_like(q)
    qs = (jnp.where(first, q, zero), jnp.where(first, zero, q))
    ones_col = (jnp.where(lane == HEAD_DIM, 1.0, 0.0).astype(BF16), jnp.where(lane == 0, 1.0, 0.0).astype(BF16))
    q0 = pl.multiple_of(qi * tq, tq)
    c0 = [ct_ref[pl.ds(2 * p + e, 1), pl.ds(q0, tq)][:, 0:1] for e in range(2)]
    row = lax.broadcasted_iota(jnp.int32, (tq, tq), 0)
    col = lax.broadcasted_iota(jnp.int32, (tq, tq), 1)
    causal = col <= row

    def scores(j, dst_ref):
        k0 = pl.multiple_of(j * tq, tq)
        kt = kt_ref[:, pl.ds(k0, tq)]
        for e in range(2):
            bias = (c0[e] - ct_ref[pl.ds(2 * p + e, 1), pl.ds(k0, tq)]) * LOG2E
            dst_ref[e] = jnp.dot(qs[e], kt, preferred_element_type=F32) + bias

    def absorb(j, src_ref, masked):
        k0 = pl.multiple_of(j * tq, tq)
        v = v_ref[pl.ds(k0, tq), :]
        vs = (jnp.where(first, v, ones_col[0]), jnp.where(first, ones_col[1], v))
        for e in range(2):
            s = src_ref[e]
            if masked:
                s = jnp.where(causal, s, -jnp.inf)
            m = m_ref[e]
            m_new = jnp.maximum(m, jnp.max(s, axis=1, keepdims=True))
            pr = jnp.exp2(s - m_new).astype(BF16)
            acc_ref[e] = jnp.exp2(m - m_new) * acc_ref[e] + jnp.dot(pr, vs[e], preferred_element_type=F32)
            m_ref[e] = m_new

    m_ref[...] = jnp.full(m_ref.shape, -jnp.inf, F32)
    acc_ref[...] = jnp.zeros(acc_ref.shape, F32)
    scores(qi, s0_ref)
    scores(jnp.maximum(qi - 1, 0), s1_ref)
    absorb(qi, s0_ref, True)

    pos = lax.broadcasted_iota(jnp.int32, (1, ct_ref.shape[1]), 1)
    n_blocks = 0
    for e in range(2):
        m_min = jnp.min(m_ref[e], axis=0, keepdims=True)
        qf = qs[e].astype(F32)
        qn2 = jnp.max(jnp.sum(qf * qf, axis=1, keepdims=True), axis=0, keepdims=True)
        kn2 = kn_ref[pl.ds(2 * p + e, 1), :][:, 0:1]
        reach = jnp.sqrt(qn2 * kn2) + BOUND_SLACK
        upper = (c0[e] - ct_ref[pl.ds(2 * p + e, 1), :]) * LOG2E + reach
        live = jnp.logical_and(upper >= m_min - DEAD_GAP, pos < q0)
        count = jnp.sum(jnp.where(live, 1.0, 0.0)).astype(jnp.int32)
        n_blocks = jnp.maximum(n_blocks, (count + (tq - 1)) // tq)

    def visit(t, _):
        j = qi - t
        nxt = jnp.maximum(j - 1, 0)

        @pl.when(t % 2 == 1)
        def _():
            scores(nxt, s0_ref)
            absorb(j, s1_ref, False)

        @pl.when(t % 2 == 0)
        def _():
            scores(nxt, s1_ref)
            absorb(j, s0_ref, False)

        return 0

    lax.fori_loop(1, n_blocks + 1, visit, 0)
    acca, accb = acc_ref[0], acc_ref[1]
    o_ref[...] = jnp.where(first, acca / acca[:, HEAD_DIM:HEAD_DIM + 1], accb / accb[:, 0:1])


def _attn_prompt(q, ktb, vb, ct, kn2, tq=512):
    l = q.shape[0]
    return pl.pallas_call(
        functools.partial(_attn_prompt_kernel, tq=tq),
        grid=(N_SLABS, l // tq),
        in_specs=[pl.BlockSpec((tq, LANES), lambda p, i: (i, p)),
                  pl.BlockSpec((LANES, l), lambda p, i: (p, 0)),
                  pl.BlockSpec((l, LANES), lambda p, i: (0, p)),
                  pl.BlockSpec((N_HEADS, l), lambda p, i: (0, 0)),
                  pl.BlockSpec((N_HEADS, LANES), lambda p, i: (0, 0))],
        out_specs=pl.BlockSpec((tq, LANES), lambda p, i: (i, p)),
        out_shape=jax.ShapeDtypeStruct((l, D_MODEL), F32),
        scratch_shapes=[pltpu.VMEM((2, tq, tq), F32), pltpu.VMEM((2, tq, tq), F32),
                        pltpu.VMEM((2, tq, 1), F32), pltpu.VMEM((2, tq, LANES), F32)],
        compiler_params=_cparams(("arbitrary", "arbitrary")),
        name="attn_prompt",
    )(q, ktb, vb, ct, kn2)


def _attn_out_kernel(o_ref, z_ref, x_ref, w_ref, g_ref, b_ref, y_ref):
    a = (o_ref[...] * _silu(z_ref[...])).astype(BF16)
    sub = jnp.dot(a, w_ref[...], preferred_element_type=F32)
    y_ref[...] = _layer_norm(ALPHA * x_ref[...] + sub, g_ref[...], b_ref[...])


def _attn_out(o, z, x, w_out, g, b, tm):
    m = x.shape[0]
    wide = pl.BlockSpec((tm, D_MODEL), lambda i: (i, 0))
    const = lambda i: (0, 0)
    return pl.pallas_call(
        _attn_out_kernel,
        grid=(m // tm,),
        in_specs=[wide, wide, wide, pl.BlockSpec((D_MODEL, D_MODEL), const),
                  pl.BlockSpec((1, D_MODEL), const), pl.BlockSpec((1, D_MODEL), const)],
        out_specs=wide,
        out_shape=jax.ShapeDtypeStruct((m, D_MODEL), F32),
        compiler_params=_cparams(("arbitrary",)),
        name="attn_out_ln",
    )(o, z, x, w_out, g, b)


def _s5_in_kernel(x_ref, w_ref, u_ref, z_ref):
    xb = x_ref[...].astype(BF16)
    u_ref[...] = jnp.dot(xb, w_ref[:, 0:D_MODEL], preferred_element_type=F32)
    z_ref[...] = jnp.dot(xb, w_ref[:, D_MODEL:2 * D_MODEL], preferred_element_type=F32)


def _s5_in_proj(x, w_in, tm):
    m = x.shape[0]
    wide = pl.BlockSpec((tm, D_MODEL), lambda i: (i, 0))
    return pl.pallas_call(
        _s5_in_kernel,
        grid=(m // tm,),
        in_specs=[wide, pl.BlockSpec((D_MODEL, 2 * D_MODEL), lambda i: (0, 0))],
        out_specs=[wide, wide],
        out_shape=[jax.ShapeDtypeStruct((m, D_MODEL), F32)] * 2,
        compiler_params=_cparams(("arbitrary",)),
        name="s5_in_proj",
    )(x, w_in)


def _s5_param_kernel(are_ref, aim_ref, ldt_ref, arer_ref, aimr_ref, ldtr_ref, bre_ref, bim_ref,
                     pre_ref, pim_ref, nre_ref, nim_ref, bbre_ref, bbim_ref):
    def abar(are, aim, ldt):
        dt = jnp.exp(ldt)
        mag = jnp.exp(are * dt)
        ang = aim * dt
        return mag * jnp.cos(ang), mag * jnp.sin(ang)

    are, aim = are_ref[...], aim_ref[...]
    ar, ai = abar(are, aim, ldt_ref[...])
    inv = 1.0 / (ar * ar + ai * ai)
    nr, ni = ar * inv, -ai * inv
    pr, pi = jnp.ones_like(ar), jnp.zeros_like(ar)
    qr, qi = pr, pi
    for t in range(S5_CHUNK):
        pre_ref[t] = pr
        pim_ref[t] = pi
        nre_ref[t] = qr
        nim_ref[t] = qi
        pr, pi = pr * ar - pi * ai, pr * ai + pi * ar
        qr, qi = qr * nr - qi * ni, qr * ni + qi * nr

    lr, li = arer_ref[...], aimr_ref[...]
    er, ei = abar(lr, li, ldtr_ref[...])
    linv = 1.0 / (lr * lr + li * li)
    fr = ((er - 1.0) * lr + ei * li) * linv
    fi = (ei * lr - (er - 1.0) * li) * linv
    br, bi = bre_ref[...], bim_ref[...]
    bbre_ref[...] = fr * br - fi * bi
    bbim_ref[...] = fr * bi + fi * br


def _s5_params(a_re, a_im, log_dt, b_re, b_im):
    g, p = a_re.shape
    rep = lambda a: jnp.repeat(a, S5_GROUP, axis=0)
    ldt = jnp.broadcast_to(log_dt[:, None], (g, p))
    b_re_t = b_re.transpose(0, 2, 1).reshape(g * S5_GROUP, p)
    b_im_t = b_im.transpose(0, 2, 1).reshape(g * S5_GROUP, p)
    tab = jax.ShapeDtypeStruct((S5_CHUNK, g, p), F32)
    big = jax.ShapeDtypeStruct((g * S5_GROUP, p), F32)
    return pl.pallas_call(
        _s5_param_kernel,
        out_shape=[tab, tab, tab, tab, big, big],
        name="s5_discretise",
    )(a_re, a_im, ldt, rep(a_re), rep(a_im), rep(ldt), b_re_t, b_im_t)


def _slab_table(t_re, t_im):
    t = t_re.shape[0]
    f = lambda a: a.reshape(t, N_SLABS, SLAB_STATE).transpose(1, 0, 2)
    return jnp.concatenate([f(t_re), f(t_im)], axis=-1)


def _s5_weights(bb_re, bb_im, c_re, c_im):
    eye = jnp.eye(GROUPS_PER_SLAB, dtype=F32)
    shape_b = (N_SLABS, GROUPS_PER_SLAB, S5_GROUP, S5_STATE)

    def in_mat(b):
        return jnp.einsum('kgjp,gh->kgjhp', b.reshape(shape_b), eye).reshape(N_SLABS, LANES, SLAB_STATE)

    def out_mat(c):
        return jnp.einsum('kgip,gh->kgphi', c.reshape(shape_b), eye).reshape(N_SLABS, SLAB_STATE, LANES)

    bbd = jnp.concatenate([in_mat(bb_re), in_mat(bb_im)], axis=-1).astype(BF16)
    cbd = jnp.concatenate([out_mat(c_re), -out_mat(c_im)], axis=1).astype(BF16)
    return bbd, cbd


def _cmul(ar, ai, br, bi):
    return ar * br - ai * bi, ar * bi + ai * br


def _s5_scan_kernel(u_ref, bbd_ref, cbd_ref, np_ref, pp_ref, d_ref, h0_ref, y_ref, hl_ref, hc_ref,
                    *, tr):
    t = S5_CHUNK
    half = SLAB_STATE

    @pl.when(pl.program_id(0) == 0)
    def _():
        hc_ref[...] = h0_ref[...]

    r = lax.broadcasted_iota(jnp.int32, (t, t), 0)
    c = lax.broadcasted_iota(jnp.int32, (t, t), 1)
    tri = jnp.where(r >= c, 1.0, 0.0).astype(BF16)

    for k in range(N_SLABS):
        cols = slice(k * LANES, (k + 1) * LANES)
        uk = u_ref[:, cols]
        bu = jnp.dot(uk.astype(BF16), bbd_ref[k], preferred_element_type=F32)
        nr, ni = np_ref[k, :, 0:half], np_ref[k, :, half:]
        pr, pi = pp_ref[k, :, 0:half], pp_ref[k, :, half:]
        ar, ai = pp_ref[k, 1:2, 0:half], pp_ref[k, 1:2, half:]
        hp = hc_ref[k]
        hpr, hpi = hp[:, 0:half], hp[:, half:]
        hh = []
        for ci in range(tr // t):
            rows = slice(ci * t, (ci + 1) * t)
            zr, zi = _cmul(nr, ni, bu[rows, 0:half], bu[rows, half:])
            zz = jnp.concatenate([zr, zi], axis=1).astype(BF16)
            s = jnp.dot(tri, zz, preferred_element_type=F32)
            gr, gi = _cmul(ar, ai, hpr, hpi)
            hr, hi = _cmul(pr, pi, s[:, 0:half] + gr, s[:, half:] + gi)
            hpr, hpi = hr[t - 1:t], hi[t - 1:t]
            hh.append(jnp.concatenate([hr, hi], axis=1).astype(BF16))
        hc_ref[k] = jnp.concatenate([hpr, hpi], axis=1)
        y = jnp.dot(jnp.concatenate(hh, axis=0), cbd_ref[k], preferred_element_type=F32)
        y_ref[:, cols] = y + d_ref[:, cols] * uk
    hl_ref[...] = hc_ref[...]


def _s5_scan(u, bbd, cbd, npow, ppow, d, h0, tr=512):
    l = u.shape[0]
    c3 = lambda i: (0, 0, 0)
    state = pl.BlockSpec((N_SLABS, 1, 2 * SLAB_STATE), c3)
    tabspec = pl.BlockSpec((N_SLABS, S5_CHUNK, 2 * SLAB_STATE), c3)
    return pl.pallas_call(
        functools.partial(_s5_scan_kernel, tr=tr),
        grid=(l // tr,),
        in_specs=[pl.BlockSpec((tr, D_MODEL), lambda i: (i, 0)),
                  pl.BlockSpec((N_SLABS, LANES, 2 * SLAB_STATE), c3),
                  pl.BlockSpec((N_SLABS, 2 * SLAB_STATE, LANES), c3),
                  tabspec, tabspec,
                  pl.BlockSpec((1, D_MODEL), lambda i: (0, 0)),
                  state],
        out_specs=[pl.BlockSpec((tr, D_MODEL), lambda i: (i, 0)), state],
        out_shape=[jax.ShapeDtypeStruct((l, D_MODEL), F32),
                   jax.ShapeDtypeStruct((N_SLABS, 1, 2 * SLAB_STATE), F32)],
        scratch_shapes=[pltpu.VMEM((N_SLABS, 1, 2 * SLAB_STATE), F32)],
        compiler_params=_cparams(("arbitrary",)),
        name="s5_scan_prompt",
    )(u, bbd, cbd, npow, ppow, d, h0)


def _s5_step_kernel(u_ref, bbd_ref, cbd_ref, pp_ref, d_ref, h0_ref, y_ref, hl_ref, *, nb, nt):
    half = SLAB_STATE
    for k in range(N_SLABS):
        cols = slice(k * LANES, (k + 1) * LANES)
        uk = u_ref[:, cols]
        bu = jnp.dot(uk.astype(BF16), bbd_ref[k], preferred_element_type=F32)
        ar, ai = pp_ref[k, 1:2, 0:half], pp_ref[k, 1:2, half:]
        h = h0_ref[k]
        hr, hi = h[:, 0:half], h[:, half:]
        hs = []
        for t in range(nt):
            rows = slice(t * nb, (t + 1) * nb)
            gr, gi = _cmul(ar, ai, hr, hi)
            hr, hi = gr + bu[rows, 0:half], gi + bu[rows, half:]
            hs.append(jnp.concatenate([hr, hi], axis=1))
        hl_ref[k] = hs[-1]
        hh = jnp.concatenate(hs, axis=0).astype(BF16)
        y_ref[:, cols] = jnp.dot(hh, cbd_ref[k], preferred_element_type=F32) + d_ref[:, cols] * uk


def _s5_steps(u_tm, bbd, cbd, ppow, d, h0, nb, nt):
    return pl.pallas_call(
        functools.partial(_s5_step_kernel, nb=nb, nt=nt),
        out_shape=[jax.ShapeDtypeStruct((nt * nb, D_MODEL), F32),
                   jax.ShapeDtypeStruct((N_SLABS, nb, 2 * SLAB_STATE), F32)],
        compiler_params=pltpu.CompilerParams(vmem_limit_bytes=VMEM_LIMIT),
        name="s5_scan_sample",
    )(u_tm, bbd, cbd, ppow, d, h0)


def _s5_out_kernel(y_ref, z_ref, x_ref, wg_ref, bg_ref, wo_ref, g_ref, b_ref, o_ref):
    g = jax.nn.gelu(y_ref[...])
    gate = jax.nn.sigmoid(jnp.dot(g.astype(BF16), wg_ref[...], preferred_element_type=F32) + bg_ref[...])
    a = (g * gate * _silu(z_ref[...])).astype(BF16)
    sub = jnp.dot(a, wo_ref[...], preferred_element_type=F32)
    o_ref[...] = _layer_norm(ALPHA * x_ref[...] + sub, g_ref[...], b_ref[...])


def _s5_out(y, z, x, w_glu, b_glu, w_out, g, b, tm):
    m = x.shape[0]
    wide = pl.BlockSpec((tm, D_MODEL), lambda i: (i, 0))
    const = lambda i: (0, 0)
    mat = pl.BlockSpec((D_MODEL, D_MODEL), const)
    vec = pl.BlockSpec((1, D_MODEL), const)
    return pl.pallas_call(
        _s5_out_kernel,
        grid=(m // tm,),
        in_specs=[wide, wide, wide, mat, vec, mat, vec, vec],
        out_specs=wide,
        out_shape=jax.ShapeDtypeStruct((m, D_MODEL), F32),
        compiler_params=_cparams(("arbitrary",)),
        name="s5_out_ln",
    )(y, z, x, w_glu, b_glu, w_out, g, b)


def _shift_pages(a, d):
    return jnp.concatenate([a[d:], jnp.zeros((d,) + a.shape[1:], a.dtype)], axis=0)


def _past_bias_kernel(pt_ref, lf_hbm, y_ref, buf_ref, sem, *, n_pages):
    b = pl.program_id(0)
    slot = b % 2

    def page_copy(bb, sl, i):
        return pltpu.make_async_copy(lf_hbm.at[pt_ref[bb * n_pages + i]], buf_ref.at[sl, i], sem.at[sl])

    def gather(bb, sl):
        def start(i, _):
            page_copy(bb, sl, i).start()
            return 0
        lax.fori_loop(0, n_pages, start, 0)

    @pl.when(b == 0)
    def _():
        gather(0, 0)

    @pl.when(b + 1 < pl.num_programs(0))
    def _():
        gather(b + 1, 1 - slot)

    def wait(i, _):
        page_copy(b, slot, i).wait()
        return 0

    lax.fori_loop(0, n_pages, wait, 0)

    x = buf_ref[slot].reshape(n_pages * N_HEADS, PAGE)
    kr = lax.broadcasted_iota(jnp.int32, (PAGE, 2 * PAGE), 0)
    kc = lax.broadcasted_iota(jnp.int32, (PAGE, 2 * PAGE), 1)
    later_or_all = jnp.where(jnp.logical_or(kc >= PAGE, kr > kc), 1.0, 0.0).astype(BF16)
    both = jnp.zeros((n_pages * N_HEADS, 2 * PAGE), F32)
    for part in _split3(x):
        both = both + jnp.dot(part, later_or_all, preferred_element_type=F32)
    excl = both[:, 0:PAGE].reshape(n_pages, N_HEADS, PAGE)
    after = both[:, PAGE:].reshape(n_pages, N_HEADS, PAGE)
    d = 1
    while d < n_pages:
        after = after + _shift_pages(after, d)
        d *= 2
    y_ref[0] = excl + _shift_pages(after, 1)


def _past_bias(page_table, cache_logf_l):
    nb, n_pages = page_table.shape
    lft = jnp.transpose(cache_logf_l, (0, 2, 1))
    return pl.pallas_call(
        functools.partial(_past_bias_kernel, n_pages=n_pages),
        grid_spec=pltpu.PrefetchScalarGridSpec(
            num_scalar_prefetch=1,
            grid=(nb,),
            in_specs=[pl.BlockSpec(memory_space=pl.ANY)],
            out_specs=pl.BlockSpec((1, n_pages, N_HEADS, PAGE), lambda b, pt: (b, 0, 0, 0)),
            scratch_shapes=[pltpu.VMEM((2, n_pages, N_HEADS, PAGE), F32), pltpu.SemaphoreType.DMA((2,))]),
        out_shape=jax.ShapeDtypeStruct((nb, n_pages, N_HEADS, PAGE), F32),
        compiler_params=_cparams(("arbitrary",)),
        name="past_bias",
    )(page_table.reshape(-1), lft)


def _decode_kernel(pt_ref, q_ref, kn_ref, vn_ref, lfn_ref, kc_hbm, vc_hbm, yb0_ref, ybn_ref, o_ref,
                   kbuf, vbuf, ksem, vsem, s_ref, live_ref, qbd_ref, m_ref, l_ref, acc_ref,
                   *, nb, nt, n_pages, group):
    b = pl.program_id(0)
    j = pl.program_id(1)
    n_steps = n_pages // group
    total = nb * n_steps
    step = b * n_steps + j
    cur = step % 2
    rows = nt * N_HEADS
    rsel = lax.broadcasted_iota(jnp.int32, (rows, N_HEADS), 0)
    csel = lax.broadcasted_iota(jnp.int32, (rows, N_HEADS), 1)
    sel = jnp.where((rsel & (N_HEADS - 1)) == csel, 1.0, 0.0).astype(BF16)

    k_slots = K_AHEAD + 1

    def pages_of(bb, jj):
        return [pt_ref[bb * n_pages + n_pages - (jj + 1) * group + g] for g in range(group)]

    def k_copies(bb, jj, st):
        return [pltpu.make_async_copy(kc_hbm.at[pg], kbuf.at[st % k_slots, g], ksem.at[st % k_slots])
                for g, pg in enumerate(pages_of(bb, jj))]

    def v_copies(bb, jj, st):
        return [pltpu.make_async_copy(vc_hbm.at[pg], vbuf.at[st % 2, g], vsem.at[st % 2])
                for g, pg in enumerate(pages_of(bb, jj))]

    def start(copies):
        for cp in copies:
            cp.start()

    def wait(copies):
        for cp in copies:
            cp.wait()

    def following(bb, jj):
        wrap = jj + 1 == n_steps
        return jnp.where(wrap, bb + 1, bb), jnp.where(wrap, 0, jj + 1)

    ahead = (b, j)
    for _ in range(K_AHEAD):
        ahead = following(*ahead)

    @pl.when(step == 0)
    def _():
        start(v_copies(0, 0, 0))
        for st in range(min(K_AHEAD, total)):
            start(k_copies(st // n_steps, st % n_steps, st))

    @pl.when(step + K_AHEAD < total)
    def _():
        start(k_copies(ahead[0], ahead[1], step + K_AHEAD))

    def scores(st, yb_ref):
        qbd = qbd_ref[...]
        tiles = []
        for g in range(group):
            sg = jnp.dot(qbd, kbuf[st % k_slots, g].astype(BF16), preferred_element_type=F32)
            tiles.append(sg + jnp.concatenate([yb_ref[0, g]] * nt, axis=0))
        return jnp.concatenate(tiles, axis=1)

    @pl.when(j == 0)
    def _():
        hrow = lax.broadcasted_iota(jnp.int32, (N_HEADS, D_MODEL), 0)
        hcol = lax.broadcasted_iota(jnp.int32, (N_HEADS, D_MODEL), 1)
        own = (hcol // HEAD_DIM) == hrow
        q = q_ref[0].astype(F32)
        qbd = [jnp.where(own, jnp.broadcast_to(q[t:t + 1], (N_HEADS, D_MODEL)), 0.0) for t in range(nt)]
        qbd_ref[...] = jnp.concatenate(qbd, axis=0).astype(BF16)

        key_row = lax.broadcasted_iota(jnp.int32, (PAGE, D_MODEL), 0)
        key_row_h = lax.broadcasted_iota(jnp.int32, (PAGE, N_HEADS), 0)
        kn, vn, lfn = kn_ref[0], vn_ref[0], lfn_ref[0]
        kpad = jnp.zeros((PAGE, D_MODEL), F32)
        vpad = jnp.zeros((PAGE, D_MODEL), F32)
        ypad = jnp.zeros((PAGE, N_HEADS), F32)
        cn = jnp.zeros((1, N_HEADS), F32)
        for t in range(nt):
            cn = cn + lfn[t:t + 1]
            kpad = jnp.where(key_row == t, jnp.broadcast_to(kn[t:t + 1], (PAGE, D_MODEL)), kpad)
            vpad = jnp.where(key_row == t, jnp.broadcast_to(vn[t:t + 1], (PAGE, D_MODEL)), vpad)
            ypad = jnp.where(key_row_h == t, jnp.broadcast_to(-cn, (PAGE, N_HEADS)), ypad)
        s = lax.dot_general(qbd_ref[...], kpad.astype(BF16), NT_DIMS, preferred_element_type=F32)
        for part in _split3(ypad):
            s = s + lax.dot_general(sel, part, NT_DIMS, preferred_element_type=F32)
        srow = lax.broadcasted_iota(jnp.int32, (rows, PAGE), 0)
        scol = lax.broadcasted_iota(jnp.int32, (rows, PAGE), 1)
        s = jnp.where(scol <= (srow // N_HEADS), s, -jnp.inf)
        m = jnp.max(s, axis=1, keepdims=True)
        pr = jnp.exp(s - m)
        m_ref[...] = m
        l_ref[...] = jnp.sum(pr, axis=1, keepdims=True)
        acc_ref[...] = jnp.dot(pr.astype(BF16), vpad.astype(BF16), preferred_element_type=F32)
        wait(k_copies(b, 0, step))
        s_ref[cur] = scores(step, yb0_ref)
        live_ref[cur] = 1

    @pl.when(live_ref[cur] == 1)
    def _():
        wait(v_copies(b, j, step))
        s = s_ref[cur]
        m = m_ref[...]
        m_new = jnp.maximum(m, jnp.max(s, axis=1, keepdims=True))
        corr = jnp.exp(m - m_new)
        pr = jnp.exp(s - m_new)
        m_ref[...] = m_new
        l_ref[...] = l_ref[...] * corr + jnp.sum(pr, axis=1, keepdims=True)
        pb = pr.astype(BF16)
        pv = jnp.zeros((rows, D_MODEL), F32)
        for g in range(group):
            pv = pv + lax.dot_general(pb[:, g * PAGE:(g + 1) * PAGE], vbuf[cur, g].astype(BF16), NT_DIMS,
                                      preferred_element_type=F32)
        acc_ref[...] = acc_ref[...] * corr + pv

    @pl.when(j + 1 < n_steps)
    def _():
        wait(k_copies(b, j + 1, step + 1))
        s_next = scores(step + 1, ybn_ref)
        s_ref[1 - cur] = s_next
        gap = jnp.max(jnp.max(s_next, axis=1, keepdims=True) - m_ref[...])
        alive = gap >= -DEAD_NAT
        live_ref[1 - cur] = alive.astype(jnp.int32)

        @pl.when(alive)
        def _():
            start(v_copies(b, j + 1, step + 1))

    @pl.when(jnp.logical_and(j + 1 == n_steps, step + 1 < total))
    def _():
        start(v_copies(b + 1, 0, step + 1))

    @pl.when(j == n_steps - 1)
    def _():
        hrow = lax.broadcasted_iota(jnp.int32, (N_HEADS, D_MODEL), 0)
        hcol = lax.broadcasted_iota(jnp.int32, (N_HEADS, D_MODEL), 1)
        own = (hcol // HEAD_DIM) == hrow
        o = acc_ref[...] / l_ref[...]
        for t in range(nt):
            blk = jnp.where(own, o[t * N_HEADS:(t + 1) * N_HEADS], 0.0)
            o_ref[0, t:t + 1, :] = jnp.sum(blk, axis=0, keepdims=True)


def _decode_attn(page_table, q, k_new, v_new, lf_new, cache_k_l, cache_v_l, ybias):
    nb, n_pages = page_table.shape
    nt = q.shape[1]
    rows = nt * N_HEADS
    n_pool = cache_k_l.shape[0]
    kc = jnp.transpose(cache_k_l, (0, 2, 3, 1)).reshape(n_pool, D_MODEL, PAGE)
    vc = jnp.transpose(cache_v_l, (0, 2, 3, 1)).reshape(n_pool, D_MODEL, PAGE)

    group = math.gcd(n_pages, DECODE_GROUP)
    n_steps = n_pages // group
    per_b = lambda b, j, pt: (b, 0, 0)
    bias_spec = lambda step_of: pl.BlockSpec((1, group, N_HEADS, PAGE),
                                             lambda b, j, pt: (b, n_steps - 1 - step_of(j), 0, 0))
    return pl.pallas_call(
        functools.partial(_decode_kernel, nb=nb, nt=nt, n_pages=n_pages, group=group),
        grid_spec=pltpu.PrefetchScalarGridSpec(
            num_scalar_prefetch=1,
            grid=(nb, n_steps),
            in_specs=[pl.BlockSpec((1, nt, D_MODEL), per_b),
                      pl.BlockSpec((1, nt, D_MODEL), per_b),
                      pl.BlockSpec((1, nt, D_MODEL), per_b),
                      pl.BlockSpec((1, nt, N_HEADS), per_b),
                      pl.BlockSpec(memory_space=pl.ANY),
                      pl.BlockSpec(memory_space=pl.ANY),
                      bias_spec(lambda j: 0),
                      bias_spec(lambda j: jnp.minimum(j + 1, n_steps - 1))],
            out_specs=pl.BlockSpec((1, nt, D_MODEL), per_b),
            scratch_shapes=[pltpu.VMEM((K_AHEAD + 1, group, D_MODEL, PAGE), F32),
                            pltpu.VMEM((2, group, D_MODEL, PAGE), F32),
                            pltpu.SemaphoreType.DMA((K_AHEAD + 1,)),
                            pltpu.SemaphoreType.DMA((2,)),
                            pltpu.VMEM((2, rows, group * PAGE), F32),
                            pltpu.SMEM((2,), jnp.int32),
                            pltpu.VMEM((rows, D_MODEL), BF16),
                            pltpu.VMEM((rows, 1), F32),
                            pltpu.VMEM((rows, 1), F32),
                            pltpu.VMEM((rows, D_MODEL), F32)]),
        out_shape=jax.ShapeDtypeStruct((nb, nt, D_MODEL), F32),
        compiler_params=_cparams(("arbitrary", "arbitrary")),
        name="decode_attn",
    )(page_table.reshape(-1), q, k_new, v_new, lf_new, kc, vc, ybias, ybias)


def _row_tile(m):
    return 512 if m % 512 == 0 else m


def kernel(x_prompt, x_sample, cache_k, cache_v, cache_logf, page_table, state_s5_re, state_s5_im,
           attn_w_in, attn_b_f, attn_w_out, s5_w_in, s5_a_re, s5_a_im, s5_log_dt,
           s5_b_re, s5_b_im, s5_c_re, s5_c_im, s5_d, s5_w_glu, s5_b_glu, s5_w_out,
           ln_g, ln_b):
    assert x_prompt.shape[0] == 1
    l = x_prompt.shape[1]
    nb, nt = x_sample.shape[:2]
    width = N_HEADS * HEAD_DIM

    w_in = attn_w_in[0]
    w_qkvz = w_in[:, :4 * width].astype(BF16)
    w_qvz = jnp.concatenate([w_in[:, 0:width], w_in[:, 2 * width:4 * width]], axis=1).astype(BF16)
    wt_kv = w_in[:, width:3 * width].T.astype(BF16)
    w_f = jnp.pad(w_in[:, 4 * width:], ((0, 0), (0, LANES - N_HEADS))).astype(BF16)
    b_f = jnp.pad(attn_b_f[0], (0, LANES - N_HEADS)).reshape(1, LANES)
    w_ao = attn_w_out[0].astype(BF16)
    w_s5in = s5_w_in[0].astype(BF16)
    w_glu = s5_w_glu[0].astype(BF16)
    b_glu = s5_b_glu[0].reshape(1, D_MODEL)
    w_so = s5_w_out[0].astype(BF16)
    g0, b0 = ln_g[0].reshape(1, D_MODEL), ln_b[0].reshape(1, D_MODEL)
    g1, b1 = ln_g[1].reshape(1, D_MODEL), ln_b[1].reshape(1, D_MODEL)
    d_skip = s5_d[0].reshape(1, D_MODEL)

    pre, pim, nre, nim, bb_re, bb_im = _s5_params(s5_a_re[0], s5_a_im[0], s5_log_dt[0],
                                                  s5_b_re[0], s5_b_im[0])
    ppow = _slab_table(pre, pim)
    npow = _slab_table(nre, nim)
    bbd, cbd = _s5_weights(bb_re, bb_im, s5_c_re[0], s5_c_im[0])

    def to_slab_state(re, im):
        f = lambda a: a.reshape(a.shape[0], N_SLABS, SLAB_STATE).transpose(1, 0, 2)
        return jnp.concatenate([f(re), f(im)], axis=-1)

    def from_slab_state(h):
        f = lambda a: a.transpose(1, 0, 2).reshape(a.shape[1], N_SLABS * GROUPS_PER_SLAB, S5_STATE)
        return f(h[..., :SLAB_STATE]), f(h[..., SLAB_STATE:])

    xp = x_prompt[0]
    tm = _row_tile(l)
    q, vb, z, kt, ktb, vt, lf, kn2 = _attn_in_proj_prompt(xp, w_qvz, wt_kv, w_f, b_f, tm)
    ct = _cumsum_t(lf)
    o = _attn_prompt(q, ktb, vb, ct, kn2)
    x1 = _attn_out(o, z, xp, w_ao, g0, b0, tm)
    u, z1 = _s5_in_proj(x1, w_s5in, tm)
    zero_state = jnp.zeros((N_SLABS, 1, 2 * SLAB_STATE), F32)
    y, h_last = _s5_scan(u, bbd, cbd, npow, ppow, d_skip, zero_state)
    y_prompt = _s5_out(y, z1, x1, w_glu, b_glu, w_so, g1, b1, tm)
    hp_re, hp_im = from_slab_state(h_last)

    ms = nb * nt
    xs = x_sample.reshape(ms, D_MODEL)
    qs, ks, vs, _, _, zs, lfs = _attn_in_proj(xs, w_qkvz, w_f, b_f, ms)
    ybias = _past_bias(page_table, cache_logf[0])
    os_ = _decode_attn(page_table, qs.reshape(nb, nt, D_MODEL), ks.reshape(nb, nt, D_MODEL),
                       vs.reshape(nb, nt, D_MODEL), lfs.reshape(nb, nt, N_HEADS),
                       cache_k[0], cache_v[0], ybias)
    xs1 = _attn_out(os_.reshape(ms, D_MODEL), zs, xs, w_ao, g0, b0, ms)
    us, zs1 = _s5_in_proj(xs1, w_s5in, ms)
    tmaj = lambda a: a.reshape(nb, nt, D_MODEL).transpose(1, 0, 2).reshape(ms, D_MODEL)
    bmaj = lambda a: a.reshape(nt, nb, D_MODEL).transpose(1, 0, 2).reshape(ms, D_MODEL)
    h0s = to_slab_state(state_s5_re[0], state_s5_im[0])
    ys_tm, hs_last = _s5_steps(tmaj(us), bbd, cbd, ppow, d_skip, h0s, nb, nt)
    y_sample = _s5_out(bmaj(ys_tm), zs1, xs1, w_glu, b_glu, w_so, g1, b1, ms)
    hs_re, hs_im = from_slab_state(hs_last)

    kv5 = lambda a, bsz, t: a.reshape(1, bsz, t, N_HEADS, HEAD_DIM)
    from_t = lambda a: a.reshape(N_HEADS, HEAD_DIM, l).transpose(2, 0, 1)[None, None]
    return (y_prompt.reshape(1, l, D_MODEL),
            y_sample.reshape(nb, nt, D_MODEL),
            from_t(kt), from_t(vt), lf.reshape(1, 1, l, N_HEADS),
            kv5(ks, nb, nt), kv5(vs, nb, nt), lfs.reshape(1, nb, nt, N_HEADS),
            hp_re[None], hp_im[None], hs_re[None], hs_im[None])
```
